```python
import jax, jax.numpy as jnp
from jax import lax
import numpy as np

D_MODEL = 1024
BATCH = 8
SEQ = 4096
DEPTH = 4

N_MIXERS = 2
CONV_KERNEL = 31
ATTN_GROUPS = ((128, 1), (512, 4), (2048, 16))
N_GROUPS = len(ATTN_GROUPS)
HEADS_PER_GROUP = 8
HEAD_DIM = 64
ATTN_WIDTH = HEADS_PER_GROUP * HEAD_DIM
QKV_WIDTH = 3 * N_GROUPS * ATTN_WIDTH
D_FF = 4 * D_MODEL
ROPE_THETA = 10000.0
EPS = 1e-6
N_CONV_LAYERS = (DEPTH + 1) // 2
N_ATTN_LAYERS = DEPTH // 2

kernel_name = "interleaved_conformer_conv_dilated_attention_trunk"


def rmsnorm(x, g):
    xf = x.astype(jnp.float32)
    y = xf * lax.rsqrt(jnp.mean(xf * xf, axis=-1, keepdims=True) + EPS)
    return (y * g.astype(jnp.float32)).astype(x.dtype)


def layernorm(x, g, b):
    xf = x.astype(jnp.float32)
    mu = jnp.mean(xf, axis=-1, keepdims=True)
    xc = xf - mu
    y = xc * lax.rsqrt(jnp.mean(xc * xc, axis=-1, keepdims=True) + EPS)
    return (y * g.astype(jnp.float32) + b.astype(jnp.float32)).astype(x.dtype)


def rope_tables(seq):
    pos = jnp.arange(seq, dtype=jnp.float32)
    inv_freq = ROPE_THETA ** (-jnp.arange(0, HEAD_DIM, 2, dtype=jnp.float32) / HEAD_DIM)
    ang = pos[:, None] * inv_freq[None, :]
    return jnp.cos(ang), jnp.sin(ang)


def apply_rope(t, cos, sin):
    tf = t.astype(jnp.float32)
    t1, t2 = jnp.split(tf, 2, axis=-1)
    c = cos[:, None, None, :]
    s = sin[:, None, None, :]
    return jnp.concatenate([t1 * c - t2 * s, t2 * c + t1 * s], axis=-1).astype(t.dtype)


def dilated_window_attention(q, k, v, window, dilation):
    b, s, h, e = q.shape
    w = window // dilation
    n_sub = s // dilation
    n_blk = -(-n_sub // w)
    pad = n_blk * w - n_sub

    def to_sub(t):
        t = t.reshape(b, n_sub, dilation, h, e).transpose(0, 2, 3, 1, 4)
        t = jnp.pad(t, ((0, 0), (0, 0), (0, 0), (0, pad), (0, 0)))
        return t.reshape(b, dilation, h, n_blk, w, e)

    def with_prev(t):
        prev = jnp.pad(t, ((0, 0), (0, 0), (0, 0), (1, 0), (0, 0), (0, 0)))[:, :, :, :-1]
        return jnp.concatenate([prev, t], axis=4)

    qs = to_sub(q)
    kb = with_prev(to_sub(k))
    vb = with_prev(to_sub(v))
    scores = jnp.einsum('brhnqe,brhnke->brhnqk', qs, kb).astype(jnp.float32) * (e ** -0.5)
    qi = jnp.arange(w)[None, :, None]
    kj = jnp.arange(2 * w)[None, None, :]
    blk = jnp.arange(n_blk)[:, None, None]
    valid = (kj >= qi) & (kj <= qi + w) & ((blk > 0) | (kj >= w))
    scores = jnp.where(valid, scores, -jnp.inf)
    m = jnp.max(scores, axis=-1, keepdims=True)
    p = jnp.exp(scores - m)
    denom = jnp.sum(p, axis=-1, keepdims=True)
    out = jnp.einsum('brhnqk,brhnke->brhnqe', (p / denom).astype(v.dtype), vb).astype(jnp.float32)
    lse = (m + jnp.log(denom))[..., 0]
    out = out.reshape(b, dilation, h, n_blk * w, e)[:, :, :, :n_sub]
    out = out.transpose(0, 3, 1, 2, 4).reshape(b, s, h, e)
    lse = lse.reshape(b, dilation, h, n_blk * w)[..., :n_sub]
    lse = lse.transpose(0, 3, 1, 2).reshape(b, s, h)
    return out, lse


def dilated_attention_mixer(hdn, w_in, q_gain, k_gain, w_out, cos, sin):
    b, s, _ = hdn.shape
    qkv = jnp.einsum('bsd,df->bsf', hdn, w_in).reshape(b, s, 3, N_GROUPS, HEADS_PER_GROUP, HEAD_DIM)
    q = apply_rope(rmsnorm(qkv[:, :, 0], q_gain), cos, sin)
    k = apply_rope(rmsnorm(qkv[:, :, 1], k_gain), cos, sin)
    v = qkv[:, :, 2]
    outs, lses = [], []
    for g, (window, dilation) in enumerate(ATTN_GROUPS):
        o, l = dilated_window_attention(q[:, :, g], k[:, :, g], v[:, :, g], window, dilation)
        outs.append(o)
        lses.append(l)
    out = jnp.stack(outs, axis=0)
    wts = jax.nn.softmax(jnp.stack(lses, axis=0), axis=0)
    mixed = jnp.sum(wts[..., None] * out, axis=0)
    mixed = mixed.reshape(b, s, ATTN_WIDTH).astype(hdn.dtype)
    return jnp.einsum('bsf,fd->bsd', mixed, w_out)


def conformer_conv_mixer(hdn, w_in, b_in, w_dw, b_dw, ln_g, ln_b, w_out, b_out):
    a = jnp.einsum('bsd,df->bsf', hdn, w_in) + b_in
    u = jax.nn.glu(a, axis=-1)
    u = lax.conv_general_dilated(
        u, w_dw[:, None, :].astype(u.dtype), window_strides=(1,),
        padding=((CONV_KERNEL - 1, 0),),
        dimension_numbers=('NWC', 'WIO', 'NWC'),
        feature_group_count=D_MODEL) + b_dw
    u = jax.nn.silu(layernorm(u, ln_g, ln_b))
    return jnp.einsum('bsd,de->bse', u, w_out) + b_out


def squared_relu_mlp(hdn, w1, w2):
    z = jax.nn.relu(jnp.einsum('bsd,df->bsf', hdn, w1))
    return jnp.einsum('bsf,fd->bsd', z * z, w2)


def setup_inputs(seed: int = 0) -> dict:
    key = jax.random.key(seed)
    ks = jax.random.split(key, 18)
    f32 = jnp.float32
    nrm = lambda k, shape, scale: jax.random.normal(k, shape, f32) * scale
    return {
        "x": nrm(ks[0], (BATCH, SEQ, D_MODEL), 1.0),
        "mixer_norm": 1.0 + nrm(ks[1], (DEPTH, D_MODEL), 0.02),
        "mlp_norm": 1.0 + nrm(ks[2], (DEPTH, D_MODEL), 0.02),
        "conv_w_in": nrm(ks[3], (N_CONV_LAYERS, D_MODEL, 2 * D_MODEL), D_MODEL ** -0.5),
        "conv_b_in": nrm(ks[4], (N_CONV_LAYERS, 2 * D_MODEL), 0.01),
        "conv_w_dw": nrm(ks[5], (N_CONV_LAYERS, CONV_KERNEL, D_MODEL), CONV_KERNEL ** -0.5),
        "conv_b_dw": nrm(ks[6], (N_CONV_LAYERS, D_MODEL), 0.01),
        "conv_ln_g": 1.0 + nrm(ks[7], (N_CONV_LAYERS, D_MODEL), 0.02),
        "conv_ln_b": nrm(ks[8], (N_CONV_LAYERS, D_MODEL), 0.01),
        "conv_w_out": nrm(ks[9], (N_CONV_LAYERS, D_MODEL, D_MODEL), D_MODEL ** -0.5),
        "conv_b_out": nrm(ks[10], (N_CONV_LAYERS, D_MODEL), 0.01),
        "attn_w_in": nrm(ks[11], (N_ATTN_LAYERS, D_MODEL, QKV_WIDTH), D_MODEL ** -0.5),
        "attn_q_norm": 1.0 + nrm(ks[12], (N_ATTN_LAYERS, HEAD_DIM), 0.02),
        "attn_k_norm": 1.0 + nrm(ks[13], (N_ATTN_LAYERS, HEAD_DIM), 0.02),
        "attn_w_out": nrm(ks[14], (N_ATTN_LAYERS, ATTN_WIDTH, D_MODEL), ATTN_WIDTH ** -0.5),
        "mlp_w1": nrm(ks[15], (DEPTH, D_MODEL, D_FF), D_MODEL ** -0.5),
        "mlp_w2": nrm(ks[16], (DEPTH, D_FF, D_MODEL), D_FF ** -0.5),
    }


def reference(x, mixer_norm, mlp_norm, conv_w_in, conv_b_in, conv_w_dw, conv_b_dw,
              conv_ln_g, conv_ln_b, conv_w_out, conv_b_out, attn_w_in, attn_q_norm,
              attn_k_norm, attn_w_out, mlp_w1, mlp_w2):
    cos, sin = rope_tables(x.shape[1])
    for i in range(DEPTH):
        hdn = rmsnorm(x, mixer_norm[i])
        j = i // N_MIXERS
        if i % N_MIXERS == 0:
            x = x + conformer_conv_mixer(hdn, conv_w_in[j], conv_b_in[j], conv_w_dw[j], conv_b_dw[j],
                                         conv_ln_g[j], conv_ln_b[j], conv_w_out[j], conv_b_out[j])
        else:
            x = x + dilated_attention_mixer(hdn, attn_w_in[j], attn_q_norm[j], attn_k_norm[j],
                                            attn_w_out[j], cos, sin)
        x = x + squared_relu_mlp(rmsnorm(x, mlp_norm[i]), mlp_w1[i], mlp_w2[i])
    return x
```

```python
import functools

import jax
import jax.numpy as jnp
from jax import lax
from jax.experimental import pallas as pl
from jax.experimental.pallas import tpu as pltpu

D_MODEL = 1024
BATCH = 8
SEQ = 4096
DEPTH = 4
CONV_KERNEL = 31
ATTN_GROUPS = ((128, 1), (512, 4), (2048, 16))
N_GROUPS = len(ATTN_GROUPS)
HEADS = 8
HEAD_DIM = 64
HALF = HEAD_DIM // 2
ATTN_WIDTH = HEADS * HEAD_DIM
D_FF = 4 * D_MODEL
ROPE_THETA = 10000.0
EPS = 1e-6

LANES = 128
WIN = 128
V7X_VMEM_LIMIT = 56 * 1024 * 1024

ROW_TILE = 512
CONV_HALO = 32
CONV_CHUNK = 64
CONV_SLABS = D_MODEL // LANES
FF_CHUNK = 1024

F32 = jnp.float32
BF16 = jnp.bfloat16
NEG = -1e30


def _params(n_axes):
    return pltpu.CompilerParams(
        dimension_semantics=("arbitrary",) * n_axes,
        vmem_limit_bytes=V7X_VMEM_LIMIT,
    )


def _const_spec(shape):
    nd = len(shape)
    return pl.BlockSpec(shape, lambda *_: (0,) * nd, pipeline_mode=pl.Buffered(1))


def _rms_rows(x, g):
    return x * lax.rsqrt(jnp.mean(x * x, axis=-1, keepdims=True) + EPS) * g


def _sigmoid(x):
    return 1.0 / (1.0 + jnp.exp(-x))


def _mlp_kernel(x_ref, g_ref, w1_ref, w2_ref, o_ref):
    x = x_ref[...]
    h = _rms_rows(x, g_ref[...]).astype(BF16)
    acc = x
    for c in range(D_FF // FF_CHUNK):
        cs = slice(c * FF_CHUNK, (c + 1) * FF_CHUNK)
        z = jnp.dot(h, w1_ref[:, cs], preferred_element_type=F32)
        z = jnp.maximum(z, 0.0)
        acc = acc + jnp.dot((z * z).astype(BF16), w2_ref[cs, :], preferred_element_type=F32)
    o_ref[...] = acc


def _mlp(xt, g, w1, w2):
    t = xt.shape[0]
    row = pl.BlockSpec((ROW_TILE, D_MODEL), lambda i: (i, 0))
    return pl.pallas_call(
        _mlp_kernel,
        grid=(t // ROW_TILE,),
        in_specs=[row, _const_spec((1, D_MODEL)), _const_spec((D_MODEL, D_FF)), _const_spec((D_FF, D_MODEL))],
        out_specs=row,
        out_shape=jax.ShapeDtypeStruct(xt.shape, F32),
        compiler_params=_params(1),
        name="mlp",
    )(xt, g.reshape(1, D_MODEL), w1, w2)


def _conv_kernel(x_ref, g_ref, win_ref, bin_ref, wdw_ref, bdw_ref, lng_ref, lnb_ref, wout_ref, bout_ref,
                 o_ref, ubuf, cbuf):
    x = x_ref[...]
    h = _rms_rows(x, g_ref[...]).astype(BF16)
    a = jnp.dot(h, win_ref[...], preferred_element_type=F32) + bin_ref[...]
    u = a[:, :D_MODEL] * _sigmoid(a[:, D_MODEL:])

    @pl.when(pl.program_id(1) == 0)
    def _():
        ubuf[:, 0:CONV_HALO, :] = jnp.zeros((CONV_SLABS, CONV_HALO, LANES), F32)

    for c in range(CONV_SLABS):
        ubuf[c, CONV_HALO:CONV_HALO + ROW_TILE, :] = u[:, c * LANES:(c + 1) * LANES]
    first_tap = CONV_HALO - (CONV_KERNEL - 1)

    def slab(c, carry):
        def chunk(r, carry):
            base = pl.multiple_of(r * CONV_CHUNK, CONV_CHUNK)
            acc = jnp.broadcast_to(bdw_ref[c], (CONV_CHUNK, LANES))
            for k in range(CONV_KERNEL):
                acc = acc + wdw_ref[c, k:k + 1, :] * ubuf[c, pl.ds(base + (first_tap + k), CONV_CHUNK), :]
            cbuf[c, pl.ds(base, CONV_CHUNK), :] = acc
            return carry
        return lax.fori_loop(0, ROW_TILE // CONV_CHUNK, chunk, carry)

    lax.fori_loop(0, CONV_SLABS, slab, 0)
    for c in range(CONV_SLABS):
        ubuf[c, 0:CONV_HALO, :] = ubuf[c, ROW_TILE:ROW_TILE + CONV_HALO, :]
    conv = jnp.concatenate([cbuf[c] for c in range(CONV_SLABS)], axis=1)
    mu = jnp.mean(conv, axis=-1, keepdims=True)
    xc = conv - mu
    y = xc * lax.rsqrt(jnp.mean(xc * xc, axis=-1, keepdims=True) + EPS) * lng_ref[...] + lnb_ref[...]
    y = (y * _sigmoid(y)).astype(BF16)
    o_ref[...] = x + jnp.dot(y, wout_ref[...], preferred_element_type=F32) + bout_ref[...]


def _conv_mixer(x3, g, w_in, b_in, w_dw, b_dw, ln_g, ln_b, w_out, b_out):
    row = pl.BlockSpec((None, ROW_TILE, D_MODEL), lambda b, j: (b, j, 0))
    vec = lambda v: v.reshape(1, -1)
    w_slabs = w_dw.reshape(CONV_KERNEL, CONV_SLABS, LANES).transpose(1, 0, 2)
    b_slabs = b_dw.reshape(CONV_SLABS, 1, LANES)
    return pl.pallas_call(
        _conv_kernel,
        grid=(BATCH, SEQ // ROW_TILE),
        in_specs=[row, _const_spec((1, D_MODEL)), _const_spec((D_MODEL, 2 * D_MODEL)),
                  _const_spec((1, 2 * D_MODEL)), _const_spec((CONV_SLABS, CONV_KERNEL, LANES)),
                  _const_spec((CONV_SLABS, 1, LANES)), _const_spec((1, D_MODEL)), _const_spec((1, D_MODEL)),
                  _const_spec((D_MODEL, D_MODEL)), _const_spec((1, D_MODEL))],
        out_specs=row,
        out_shape=jax.ShapeDtypeStruct(x3.shape, F32),
        scratch_shapes=[pltpu.VMEM((CONV_SLABS, ROW_TILE + CONV_HALO, LANES), F32),
                        pltpu.VMEM((CONV_SLABS, ROW_TILE, LANES), F32)],
        compiler_params=_params(2),
        name="conv_mixer",
    )(x3, vec(g), w_in, vec(b_in), w_slabs, b_slabs, vec(ln_g), vec(ln_b), w_out, vec(b_out))


def _group_tiles(dilation):
    n_sub = SEQ // dilation
    tl = min(n_sub, ROW_TILE)
    return ROW_TILE // tl, tl


def _qkv_kernel(x_ref, g_ref, wt_ref, qg_ref, kg_ref, cos_ref, sin_ref, qt_ref, k_ref, vt_ref, *, rb_count, tl):
    reps = tl // LANES
    widen = lambda t: jnp.concatenate([t] * reps, axis=1)
    gains = (widen(qg_ref[...]), widen(kg_ref[...]))
    for rb in range(rb_count):
        x = x_ref[:, rb * D_MODEL:(rb + 1) * D_MODEL]
        h = _rms_rows(x, g_ref[...]).astype(BF16)
        y = lax.dot_general(wt_ref[...], h, (((1,), (1,)), ((), ())), preferred_element_type=F32)
        cos = cos_ref[rb]
        sin = sin_ref[rb]
        rot = []
        for s in range(2):
            pieces = []
            for hh in range(HEADS):
                t = y[s * ATTN_WIDTH + hh * HEAD_DIM: s * ATTN_WIDTH + (hh + 1) * HEAD_DIM, :]
                rs = lax.rsqrt(jnp.mean(t * t, axis=0, keepdims=True) + EPS)
                tn = t * rs * gains[s]
                t1, t2 = tn[:HALF], tn[HALF:]
                pieces += [t1 * cos - t2 * sin, t2 * cos + t1 * sin]
            rot.append(jnp.concatenate(pieces, axis=0))
        qt_ref[rb] = rot[0].astype(BF16)
        k_ref[rb] = rot[1].T.astype(BF16)
        vt_ref[rb] = y[2 * ATTN_WIDTH:, :].astype(BF16)


def _qkv_project(x3, g, wt, qg, kg, cos_t, sin_t, dilation):
    rb, tl = _group_tiles(dilation)
    n_sub = SEQ // dilation
    xv = x3.reshape(BATCH, n_sub, dilation * D_MODEL)
    t_spec = pl.BlockSpec((None, rb, ATTN_WIDTH, tl), lambda b, r, l: (b, r, 0, l))
    r_spec = pl.BlockSpec((None, rb, tl, ATTN_WIDTH), lambda b, r, l: (b, r, l, 0))
    tab = pl.BlockSpec((rb, HALF, tl), lambda b, r, l: (r, 0, l))
    return pl.pallas_call(
        functools.partial(_qkv_kernel, rb_count=rb, tl=tl),
        grid=(BATCH, dilation // rb, n_sub // tl),
        in_specs=[pl.BlockSpec((None, tl, rb * D_MODEL), lambda b, r, l: (b, l, r)),
                  _const_spec((1, D_MODEL)), _const_spec((3 * ATTN_WIDTH, D_MODEL)),
                  _const_spec((HEAD_DIM, LANES)), _const_spec((HEAD_DIM, LANES)), tab, tab],
        out_specs=[t_spec, r_spec, t_spec],
        out_shape=[jax.ShapeDtypeStruct((BATCH, dilation, ATTN_WIDTH, n_sub), BF16),
                   jax.ShapeDtypeStruct((BATCH, dilation, n_sub, ATTN_WIDTH), BF16),
                   jax.ShapeDtypeStruct((BATCH, dilation, ATTN_WIDTH, n_sub), BF16)],
        compiler_params=_params(3),
        name=f"qkv_d{dilation}",
    )(xv, g.reshape(1, D_MODEL), wt, qg, kg, cos_t, sin_t)


def _attn_kernel(qt_ref, kc_ref, kp_ref, vc_ref, vp_ref, o_ref, lse_ref, *, rb_count, qb):
    kj = lax.broadcasted_iota(jnp.int32, (2 * WIN, WIN), 0)
    qi = lax.broadcasted_iota(jnp.int32, (2 * WIN, WIN), 1)
    band = (kj >= qi) & (kj <= qi + WIN)
    bias_rest = jnp.where(band, 0.0, NEG).astype(F32)
    bias_first = jnp.where(band & (kj >= WIN), 0.0, NEG).astype(F32)
    bias_head = jnp.where(pl.program_id(2) > 0, bias_rest, bias_first)
    zeros = jnp.zeros((HEAD_DIM, WIN), BF16)
    pad = jnp.zeros((LANES - HEADS, WIN), F32)

    for rb in range(rb_count):
        for i in range(qb // WIN):
            cur = slice(i * WIN, (i + 1) * WIN)
            prev = slice((i - 1) * WIN, i * WIN)
            bias = bias_head if i == 0 else bias_rest
            bias2 = jnp.concatenate([bias, bias], axis=1)
            outs, lses = [], []
            for p in range(HEADS // 2):
                cols = slice(p * 2 * HEAD_DIM, (p + 1) * 2 * HEAD_DIM)
                if i == 0:
                    k_prev = kp_ref[rb, :, cols]
                    v_prev = vp_ref[rb, cols, :]
                else:
                    k_prev = kc_ref[rb, prev, cols]
                    v_prev = vc_ref[rb, cols, prev]
                kk = jnp.concatenate([k_prev, kc_ref[rb, cur, cols]], axis=0)
                vv = jnp.concatenate([v_prev, vc_ref[rb, cols, cur]], axis=1)
                q2 = qt_ref[rb, cols, cur]
                qbd = jnp.concatenate(
                    [jnp.concatenate([q2[:HEAD_DIM], zeros], axis=1),
                     jnp.concatenate([zeros, q2[HEAD_DIM:]], axis=1)], axis=0)
                s = jnp.dot(kk, qbd, preferred_element_type=F32) + bias2
                m = jnp.max(s, axis=0, keepdims=True)
                e = jnp.exp(s - m)
                den = jnp.sum(e, axis=0, keepdims=True)
                o2 = jnp.dot(vv, e.astype(BF16), preferred_element_type=F32)
                inv = 1.0 / den
                outs += [o2[:HEAD_DIM, :WIN] * inv[:, :WIN], o2[HEAD_DIM:, WIN:] * inv[:, WIN:]]
                lse = m + jnp.log(den)
                lses += [lse[:, :WIN], lse[:, WIN:]]
            o_t = jnp.concatenate(outs, axis=0)
            o_ref[cur, rb * ATTN_WIDTH:(rb + 1) * ATTN_WIDTH] = o_t.T
            lse_t = jnp.concatenate(lses + [pad], axis=0)
            lse_ref[cur, rb * LANES:(rb + 1) * LANES] = lse_t.T


def _attention(qt, k, vt, dilation):
    rb, qb = _group_tiles(dilation)
    n_sub = SEQ // dilation
    per = qb // WIN
    prev_blk = lambda j: jnp.maximum(j * per - 1, 0)
    out, lse = pl.pallas_call(
        functools.partial(_attn_kernel, rb_count=rb, qb=qb),
        grid=(BATCH, dilation // rb, n_sub // qb),
        in_specs=[pl.BlockSpec((None, rb, ATTN_WIDTH, qb), lambda b, r, j: (b, r, 0, j)),
                  pl.BlockSpec((None, rb, qb, ATTN_WIDTH), lambda b, r, j: (b, r, j, 0)),
                  pl.BlockSpec((None, rb, WIN, ATTN_WIDTH), lambda b, r, j: (b, r, prev_blk(j), 0)),
                  pl.BlockSpec((None, rb, ATTN_WIDTH, qb), lambda b, r, j: (b, r, 0, j)),
                  pl.BlockSpec((None, rb, ATTN_WIDTH, WIN), lambda b, r, j: (b, r, 0, prev_blk(j)))],
        out_specs=[pl.BlockSpec((None, qb, rb * ATTN_WIDTH), lambda b, r, j: (b, j, r)),
                   pl.BlockSpec((None, qb, rb * LANES), lambda b, r, j: (b, j, r))],
        out_shape=[jax.ShapeDtypeStruct((BATCH, n_sub, dilation * ATTN_WIDTH), F32),
                   jax.ShapeDtypeStruct((BATCH, n_sub, dilation * LANES), F32)],
        compiler_params=_params(3),
        name=f"attn_d{dilation}",
    )(qt, k, k, vt, vt)
    return out.reshape(BATCH * SEQ, ATTN_WIDTH), lse.reshape(BATCH * SEQ, LANES)


def _combine_kernel(x_ref, o0_ref, o1_ref, o2_ref, l0_ref, l1_ref, l2_ref, ex_ref, w_ref, out_ref):
    ls = [l0_ref[...], l1_ref[...], l2_ref[...]]
    m = jnp.maximum(jnp.maximum(ls[0], ls[1]), ls[2])
    es = [jnp.exp(l - m) for l in ls]
    inv = 1.0 / (es[0] + es[1] + es[2])
    mixed = jnp.zeros((ROW_TILE, ATTN_WIDTH), F32)
    for e, o_ref in zip(es, (o0_ref, o1_ref, o2_ref)):
        w = e * inv
        hi = w.astype(BF16)
        lo = (w - hi.astype(F32)).astype(BF16)
        wide = (jnp.dot(hi, ex_ref[...], preferred_element_type=F32)
                + jnp.dot(lo, ex_ref[...], preferred_element_type=F32))
        mixed = mixed + wide * o_ref[...]
    out_ref[...] = x_ref[...] + jnp.dot(mixed.astype(BF16), w_ref[...], preferred_element_type=F32)


def _combine_project(xt, outs, lses, expand, w_out):
    t = xt.shape[0]
    row = lambda w: pl.BlockSpec((ROW_TILE, w), lambda i: (i, 0))
    return pl.pallas_call(
        _combine_kernel,
        grid=(t // ROW_TILE,),
        in_specs=[row(D_MODEL)] + [row(ATTN_WIDTH)] * N_GROUPS + [row(LANES)] * N_GROUPS
                 + [_const_spec((LANES, ATTN_WIDTH)), _const_spec((ATTN_WIDTH, D_MODEL))],
        out_specs=row(D_MODEL),
        out_shape=jax.ShapeDtypeStruct(xt.shape, F32),
        compiler_params=_params(1),
        name="attn_combine",
    )(xt, *outs, *lses, expand, w_out)


def _rope_tables():
    pos = jnp.arange(SEQ, dtype=F32)
    inv_freq = ROPE_THETA ** (-jnp.arange(0, HEAD_DIM, 2, dtype=F32) / HEAD_DIM)
    ang = pos[:, None] * inv_freq[None, :]
    return jnp.cos(ang), jnp.sin(ang)


def _dilated_table(tab, dilation):
    return tab.reshape(SEQ // dilation, dilation, HALF).transpose(1, 2, 0)


def _attention_layer(xt, g, w_in, q_gain, k_gain, w_out, cos, sin):
    x3 = xt.reshape(BATCH, SEQ, D_MODEL)
    w4 = w_in.reshape(D_MODEL, 3, N_GROUPS, ATTN_WIDTH)
    lane_bcast = lambda v: jnp.broadcast_to(v.reshape(HEAD_DIM, 1), (HEAD_DIM, LANES))
    qg = lane_bcast(q_gain * (HEAD_DIM ** -0.5))
    kg = lane_bcast(k_gain)
    head = jnp.arange(LANES)[:, None]
    lane = jnp.arange(ATTN_WIDTH)[None, :]
    expand = (lane // HEAD_DIM == head).astype(BF16)
    outs, lses = [], []
    for gi, (_, dilation) in enumerate(ATTN_GROUPS):
        wt = w4[:, :, gi, :].reshape(D_MODEL, 3 * ATTN_WIDTH).T.astype(BF16)
        qt, k, vt = _qkv_project(x3, g, wt, qg, kg, _dilated_table(cos, dilation),
                                 _dilated_table(sin, dilation), dilation)
        o, l = _attention(qt, k, vt, dilation)
        outs.append(o)
        lses.append(l)
    return _combine_project(xt, outs, lses, expand, w_out.astype(BF16))


def kernel(x, mixer_norm, mlp_norm, conv_w_in, conv_b_in, conv_w_dw, conv_b_dw, conv_ln_g, conv_ln_b,
           conv_w_out, conv_b_out, attn_w_in, attn_q_norm, attn_k_norm, attn_w_out, mlp_w1, mlp_w2):
    assert x.shape == (BATCH, SEQ, D_MODEL) and x.dtype == F32
    cos, sin = _rope_tables()
    xt = x.reshape(BATCH * SEQ, D_MODEL)
    for i in range(DEPTH):
        j = i // 2
        if i % 2 == 0:
            x3 = _conv_mixer(xt.reshape(BATCH, SEQ, D_MODEL), mixer_norm[i], conv_w_in[j].astype(BF16),
                             conv_b_in[j], conv_w_dw[j], conv_b_dw[j], conv_ln_g[j], conv_ln_b[j],
                             conv_w_out[j].astype(BF16), conv_b_out[j])
            xt = x3.reshape(BATCH * SEQ, D_MODEL)
        else:
            xt = _attention_layer(xt, mixer_norm[i], attn_w_in[j], attn_q_norm[j], attn_k_norm[j],
                                  attn_w_out[j], cos, sin)
        xt = _mlp(xt, mlp_norm[i], mlp_w1[i].astype(BF16), mlp_w2[i].astype(BF16))
    return xt.reshape(BATCH, SEQ, D_MODEL)
```

```python
import functools

import jax
import jax.numpy as jnp
from jax import lax
from jax.experimental import pallas as pl
from jax.experimental.pallas import tpu as pltpu

D_MODEL = 1024
BATCH = 8
SEQ = 4096
DEPTH = 4
CONV_KERNEL = 31
ATTN_GROUPS = ((128, 1), (512, 4), (2048, 16))
N_GROUPS = len(ATTN_GROUPS)
HEADS = 8
HEAD_DIM = 64
HALF = HEAD_DIM // 2
ATTN_WIDTH = HEADS * HEAD_DIM
D_FF = 4 * D_MODEL
ROPE_THETA = 10000.0
EPS = 1e-6

LANES = 128
WIN = 128
V7X_VMEM_LIMIT = 56 * 1024 * 1024

ROW_TILE = 512
CONV_HALO = 32
CONV_CHUNK = 64
CONV_SLABS = D_MODEL // LANES
FF_CHUNK = 1024

F32 = jnp.float32
BF16 = jnp.bfloat16
NEG = -1e30
LN2 = 0.6931471805599453
LOG2E = 1.4426950408889634


def _params(n_axes):
    return pltpu.CompilerParams(
        dimension_semantics=("arbitrary",) * n_axes,
        vmem_limit_bytes=V7X_VMEM_LIMIT,
    )


def _const_spec(shape):
    nd = len(shape)
    return pl.BlockSpec(shape, lambda *_: (0,) * nd, pipeline_mode=pl.Buffered(1))


def _rms_rows(x, g):
    return x * lax.rsqrt(jnp.mean(x * x, axis=-1, keepdims=True) + EPS) * g


def _sigmoid(x):
    return 1.0 / (1.0 + jnp.exp(-x))


def _mlp_kernel(x_ref, g_ref, w1_ref, w2_ref, o_ref):
    x = x_ref[...]
    h = _rms_rows(x, g_ref[...]).astype(BF16)
    acc = x
    for c in range(D_FF // FF_CHUNK):
        cs = slice(c * FF_CHUNK, (c + 1) * FF_CHUNK)
        z = jnp.dot(h, w1_ref[:, cs], preferred_element_type=F32)
        z = jnp.maximum(z, 0.0)
        acc = acc + jnp.dot((z * z).astype(BF16), w2_ref[cs, :], preferred_element_type=F32)
    o_ref[...] = acc


def _mlp(xt, g, w1, w2):
    t = xt.shape[0]
    row = pl.BlockSpec((ROW_TILE, D_MODEL), lambda i: (i, 0))
    return pl.pallas_call(
        _mlp_kernel,
        grid=(t // ROW_TILE,),
        in_specs=[row, _const_spec((1, D_MODEL)), _const_spec((D_MODEL, D_FF)), _const_spec((D_FF, D_MODEL))],
        out_specs=row,
        out_shape=jax.ShapeDtypeStruct(xt.shape, F32),
        compiler_params=_params(1),
        name="mlp",
    )(xt, g.reshape(1, D_MODEL), w1, w2)


def _conv_kernel(x_ref, g_ref, win_ref, bin_ref, wdw_ref, bdw_ref, lng_ref, lnb_ref, wout_ref, bout_ref,
                 o_ref, ubuf, cbuf):
    x = x_ref[...]
    h = _rms_rows(x, g_ref[...]).astype(BF16)
    a = jnp.dot(h, win_ref[...], preferred_element_type=F32) + bin_ref[...]
    u = a[:, :D_MODEL] * _sigmoid(a[:, D_MODEL:])

    @pl.when(pl.program_id(1) == 0)
    def _():
        ubuf[:, 0:CONV_HALO, :] = jnp.zeros((CONV_SLABS, CONV_HALO, LANES), F32)

    for c in range(CONV_SLABS):
        ubuf[c, CONV_HALO:CONV_HALO + ROW_TILE, :] = u[:, c * LANES:(c + 1) * LANES]
    first_tap = CONV_HALO - (CONV_KERNEL - 1)

    def slab(c, carry):
        def chunk(r, carry):
            base = pl.multiple_of(r * CONV_CHUNK, CONV_CHUNK)
            acc = jnp.broadcast_to(bdw_ref[c], (CONV_CHUNK, LANES))
            for k in range(CONV_KERNEL):
                acc = acc + wdw_ref[c, k:k + 1, :] * ubuf[c, pl.ds(base + (first_tap + k), CONV_CHUNK), :]
            cbuf[c, pl.ds(base, CONV_CHUNK), :] = acc
            return carry
        return lax.fori_loop(0, ROW_TILE // CONV_CHUNK, chunk, carry)

    lax.fori_loop(0, CONV_SLABS, slab, 0)
    for c in range(CONV_SLABS):
        ubuf[c, 0:CONV_HALO, :] = ubuf[c, ROW_TILE:ROW_TILE + CONV_HALO, :]
    conv = jnp.concatenate([cbuf[c] for c in range(CONV_SLABS)], axis=1)
    mu = jnp.mean(conv, axis=-1, keepdims=True)
    xc = conv - mu
    y = xc * lax.rsqrt(jnp.mean(xc * xc, axis=-1, keepdims=True) + EPS) * lng_ref[...] + lnb_ref[...]
    y = (y * _sigmoid(y)).astype(BF16)
    o_ref[...] = x + jnp.dot(y, wout_ref[...], preferred_element_type=F32) + bout_ref[...]


def _conv_mixer(x3, g, w_in, b_in, w_dw, b_dw, ln_g, ln_b, w_out, b_out):
    row = pl.BlockSpec((None, ROW_TILE, D_MODEL), lambda b, j: (b, j, 0))
    vec = lambda v: v.reshape(1, -1)
    w_slabs = w_dw.reshape(CONV_KERNEL, CONV_SLABS, LANES).transpose(1, 0, 2)
    b_slabs = b_dw.reshape(CONV_SLABS, 1, LANES)
    return pl.pallas_call(
        _conv_kernel,
        grid=(BATCH, SEQ // ROW_TILE),
        in_specs=[row, _const_spec((1, D_MODEL)), _const_spec((D_MODEL, 2 * D_MODEL)),
                  _const_spec((1, 2 * D_MODEL)), _const_spec((CONV_SLABS, CONV_KERNEL, LANES)),
                  _const_spec((CONV_SLABS, 1, LANES)), _const_spec((1, D_MODEL)), _const_spec((1, D_MODEL)),
                  _const_spec((D_MODEL, D_MODEL)), _const_spec((1, D_MODEL))],
        out_specs=row,
        out_shape=jax.ShapeDtypeStruct(x3.shape, F32),
        scratch_shapes=[pltpu.VMEM((CONV_SLABS, ROW_TILE + CONV_HALO, LANES), F32),
                        pltpu.VMEM((CONV_SLABS, ROW_TILE, LANES), F32)],
        compiler_params=_params(2),
        name="conv_mixer",
    )(x3, vec(g), w_in, vec(b_in), w_slabs, b_slabs, vec(ln_g), vec(ln_b), w_out, vec(b_out))


def _group_tiles(dilation):
    n_sub = SEQ // dilation
    tl = min(n_sub, ROW_TILE)
    return ROW_TILE // tl, tl


def _qkv_kernel(x_ref, g_ref, wt_ref, qg_ref, kg_ref, cos_ref, sin_ref, qt_ref, k_ref, vt_ref, *, rb_count, tl):
    reps = tl // LANES
    widen = lambda t: jnp.concatenate([t] * reps, axis=1)
    gains = (widen(qg_ref[...]), widen(kg_ref[...]))
    for rb in range(rb_count):
        x = x_ref[:, rb * D_MODEL:(rb + 1) * D_MODEL]
        h = _rms_rows(x, g_ref[...]).astype(BF16)
        ys = [lax.dot_general(wt_ref[s * ATTN_WIDTH:(s + 1) * ATTN_WIDTH, :], h, (((1,), (1,)), ((), ())),
                              preferred_element_type=F32) for s in range(3)]
        cos = cos_ref[rb]
        sin = sin_ref[rb]
        rot = []
        for s in range(2):
            pieces = []
            for hh in range(HEADS):
                t = ys[s][hh * HEAD_DIM:(hh + 1) * HEAD_DIM, :]
                rs = lax.rsqrt(jnp.mean(t * t, axis=0, keepdims=True) + EPS)
                tn = t * rs * gains[s]
                t1, t2 = tn[:HALF], tn[HALF:]
                pieces += [t1 * cos - t2 * sin, t2 * cos + t1 * sin]
            rot.append(jnp.concatenate(pieces, axis=0))
        qt_ref[rb] = rot[0].astype(BF16)
        k_ref[rb] = rot[1].T.astype(BF16)
        vt_ref[rb] = ys[2].astype(BF16)


def _qkv_project(x3, g, wt, qg, kg, cos_t, sin_t, dilation):
    rb, tl = _group_tiles(dilation)
    n_sub = SEQ // dilation
    xv = x3.reshape(BATCH, n_sub, dilation * D_MODEL)
    t_spec = pl.BlockSpec((None, rb, ATTN_WIDTH, tl), lambda b, r, l: (b, r, 0, l))
    r_spec = pl.BlockSpec((None, rb, tl, ATTN_WIDTH), lambda b, r, l: (b, r, l, 0))
    tab = pl.BlockSpec((rb, HALF, tl), lambda b, r, l: (r, 0, l))
    return pl.pallas_call(
        functools.partial(_qkv_kernel, rb_count=rb, tl=tl),
        grid=(BATCH, dilation // rb, n_sub // tl),
        in_specs=[pl.BlockSpec((None, tl, rb * D_MODEL), lambda b, r, l: (b, l, r)),
                  _const_spec((1, D_MODEL)), _const_spec((3 * ATTN_WIDTH, D_MODEL)),
                  _const_spec((HEAD_DIM, LANES)), _const_spec((HEAD_DIM, LANES)), tab, tab],
        out_specs=[t_spec, r_spec, t_spec],
        out_shape=[jax.ShapeDtypeStruct((BATCH, dilation, ATTN_WIDTH, n_sub), BF16),
                   jax.ShapeDtypeStruct((BATCH, dilation, n_sub, ATTN_WIDTH), BF16),
                   jax.ShapeDtypeStruct((BATCH, dilation, ATTN_WIDTH, n_sub), BF16)],
        compiler_params=_params(3),
        name=f"qkv_d{dilation}",
    )(xv, g.reshape(1, D_MODEL), wt, qg, kg, cos_t, sin_t)


def _attn_kernel(qt_ref, kc_ref, kp_ref, vc_ref, vp_ref, o_ref, lse_ref, s_scr, p_scr, *, rb_count, qb):
    kj = lax.broadcasted_iota(jnp.int32, (2 * WIN, WIN), 0)
    qi = lax.broadcasted_iota(jnp.int32, (2 * WIN, WIN), 1)
    band = (kj >= qi) & (kj <= qi + WIN)
    bias_rest = jnp.where(band, 0.0, NEG).astype(F32)
    bias_first = jnp.where(band & (kj >= WIN), 0.0, NEG).astype(F32)
    bias_head = jnp.where(pl.program_id(2) > 0, bias_rest, bias_first)
    zeros = jnp.zeros((HEAD_DIM, WIN), BF16)
    pad = jnp.zeros((LANES - HEADS, WIN), F32)
    units = [(rb, i, p) for rb in range(rb_count) for i in range(qb // WIN) for p in range(HEADS // 2)]

    def slices(i, p):
        return (slice(i * WIN, (i + 1) * WIN), slice((i - 1) * WIN, i * WIN),
                slice(p * 2 * HEAD_DIM, (p + 1) * 2 * HEAD_DIM))

    for n, (rb, i, p) in enumerate(units):
        cur, prev, cols = slices(i, p)
        k_prev = kp_ref[rb, :, cols] if i == 0 else kc_ref[rb, prev, cols]
        kk = jnp.concatenate([k_prev, kc_ref[rb, cur, cols]], axis=0)
        q2 = qt_ref[rb, cols, cur]
        qbd = jnp.concatenate(
            [jnp.concatenate([q2[:HEAD_DIM], zeros], axis=1),
             jnp.concatenate([zeros, q2[HEAD_DIM:]], axis=1)], axis=0)
        bias = bias_head if i == 0 else bias_rest
        s_scr[n] = (jnp.dot(kk, qbd, preferred_element_type=F32)
                    + jnp.concatenate([bias, bias], axis=1))

    stats = []
    for n in range(len(units)):
        s = s_scr[n]
        m = jnp.max(s, axis=0, keepdims=True)
        e = jnp.exp2(s - m)
        p_scr[n] = e.astype(BF16)
        stats.append((m, jnp.sum(e, axis=0, keepdims=True)))

    outs, lses = {}, {}
    for n, (rb, i, p) in enumerate(units):
        cur, prev, cols = slices(i, p)
        v_prev = vp_ref[rb, cols, :] if i == 0 else vc_ref[rb, cols, prev]
        vv = jnp.concatenate([v_prev, vc_ref[rb, cols, cur]], axis=1)
        o2 = jnp.dot(vv, p_scr[n], preferred_element_type=F32)
        m, den = stats[n]
        inv = 1.0 / den
        outs.setdefault((rb, i), []).extend(
            [o2[:HEAD_DIM, :WIN] * inv[:, :WIN], o2[HEAD_DIM:, WIN:] * inv[:, WIN:]])
        lse = m * LN2 + jnp.log(den)
        lses.setdefault((rb, i), []).extend([lse[:, :WIN], lse[:, WIN:]])
    for (rb, i), pieces in outs.items():
        cur = slice(i * WIN, (i + 1) * WIN)
        o_ref[cur, rb * ATTN_WIDTH:(rb + 1) * ATTN_WIDTH] = jnp.concatenate(pieces, axis=0).T
        lse_ref[cur, rb * LANES:(rb + 1) * LANES] = jnp.concatenate(lses[(rb, i)] + [pad], axis=0).T


def _attention(qt, k, vt, dilation):
    rb, qb = _group_tiles(dilation)
    n_sub = SEQ // dilation
    per = qb // WIN
    prev_blk = lambda j: jnp.maximum(j * per - 1, 0)
    out, lse = pl.pallas_call(
        functools.partial(_attn_kernel, rb_count=rb, qb=qb),
        grid=(BATCH, dilation // rb, n_sub // qb),
        in_specs=[pl.BlockSpec((None, rb, ATTN_WIDTH, qb), lambda b, r, j: (b, r, 0, j)),
                  pl.BlockSpec((None, rb, qb, ATTN_WIDTH), lambda b, r, j: (b, r, j, 0)),
                  pl.BlockSpec((None, rb, WIN, ATTN_WIDTH), lambda b, r, j: (b, r, prev_blk(j), 0)),
                  pl.BlockSpec((None, rb, ATTN_WIDTH, qb), lambda b, r, j: (b, r, 0, j)),
                  pl.BlockSpec((None, rb, ATTN_WIDTH, WIN), lambda b, r, j: (b, r, 0, prev_blk(j)))],
        out_specs=[pl.BlockSpec((None, qb, rb * ATTN_WIDTH), lambda b, r, j: (b, j, r)),
                   pl.BlockSpec((None, qb, rb * LANES), lambda b, r, j: (b, j, r))],
        out_shape=[jax.ShapeDtypeStruct((BATCH, n_sub, dilation * ATTN_WIDTH), F32),
                   jax.ShapeDtypeStruct((BATCH, n_sub, dilation * LANES), F32)],
        scratch_shapes=[pltpu.VMEM((rb * per * (HEADS // 2), 2 * WIN, 2 * WIN), F32),
                        pltpu.VMEM((rb * per * (HEADS // 2), 2 * WIN, 2 * WIN), BF16)],
        compiler_params=_params(3),
        name=f"attn_d{dilation}",
    )(qt, k, k, vt, vt)
    return out.reshape(BATCH * SEQ, ATTN_WIDTH), lse.reshape(BATCH * SEQ, LANES)


def _combine_kernel(x_ref, o0_ref, o1_ref, o2_ref, l0_ref, l1_ref, l2_ref, ex_ref, w_ref, out_ref):
    ls = [l0_ref[...], l1_ref[...], l2_ref[...]]
    m = jnp.maximum(jnp.maximum(ls[0], ls[1]), ls[2])
    es = [jnp.exp(l - m) for l in ls]
    inv = 1.0 / (es[0] + es[1] + es[2])
    mixed = jnp.zeros((ROW_TILE, ATTN_WIDTH), F32)
    for e, o_ref in zip(es, (o0_ref, o1_ref, o2_ref)):
        w = e * inv
        hi = w.astype(BF16)
        lo = (w - hi.astype(F32)).astype(BF16)
        wide = (jnp.dot(hi, ex_ref[...], preferred_element_type=F32)
                + jnp.dot(lo, ex_ref[...], preferred_element_type=F32))
        mixed = mixed + wide * o_ref[...]
    out_ref[...] = x_ref[...] + jnp.dot(mixed.astype(BF16), w_ref[...], preferred_element_type=F32)


def _combine_project(xt, outs, lses, expand, w_out):
    t = xt.shape[0]
    row = lambda w: pl.BlockSpec((ROW_TILE, w), lambda i: (i, 0))
    return pl.pallas_call(
        _combine_kernel,
        grid=(t // ROW_TILE,),
        in_specs=[row(D_MODEL)] + [row(ATTN_WIDTH)] * N_GROUPS + [row(LANES)] * N_GROUPS
                 + [_const_spec((LANES, ATTN_WIDTH)), _const_spec((ATTN_WIDTH, D_MODEL))],
        out_specs=row(D_MODEL),
        out_shape=jax.ShapeDtypeStruct(xt.shape, F32),
        compiler_params=_params(1),
        name="attn_combine",
    )(xt, *outs, *lses, expand, w_out)


def _rope_tables():
    pos = jnp.arange(SEQ, dtype=F32)
    inv_freq = ROPE_THETA ** (-jnp.arange(0, HEAD_DIM, 2, dtype=F32) / HEAD_DIM)
    ang = pos[:, None] * inv_freq[None, :]
    return jnp.cos(ang), jnp.sin(ang)


def _dilated_table(tab, dilation):
    return tab.reshape(SEQ // dilation, dilation, HALF).transpose(1, 2, 0)


def _attention_layer(xt, g, w_in, q_gain, k_gain, w_out, cos, sin):
    x3 = xt.reshape(BATCH, SEQ, D_MODEL)
    w4 = w_in.reshape(D_MODEL, 3, N_GROUPS, ATTN_WIDTH)
    lane_bcast = lambda v: jnp.broadcast_to(v.reshape(HEAD_DIM, 1), (HEAD_DIM, LANES))
    qg = lane_bcast(q_gain * (HEAD_DIM ** -0.5 * LOG2E))
    kg = lane_bcast(k_gain)
    head = jnp.arange(LANES)[:, None]
    lane = jnp.arange(ATTN_WIDTH)[None, :]
    expand = (lane // HEAD_DIM == head).astype(BF16)
    outs, lses = [], []
    for gi, (_, dilation) in enumerate(ATTN_GROUPS):
        wt = w4[:, :, gi, :].reshape(D_MODEL, 3 * ATTN_WIDTH).T.astype(BF16)
        qt, k, vt = _qkv_project(x3, g, wt, qg, kg, _dilated_table(cos, dilation),
                                 _dilated_table(sin, dilation), dilation)
        o, l = _attention(qt, k, vt, dilation)
        outs.append(o)
        lses.append(l)
    return _combine_project(xt, outs, lses, expand, w_out.astype(BF16))


def kernel(x, mixer_norm, mlp_norm, conv_w_in, conv_b_in, conv_w_dw, conv_b_dw, conv_ln_g, conv_ln_b,
           conv_w_out, conv_b_out, attn_w_in, attn_q_norm, attn_k_norm, attn_w_out, mlp_w1, mlp_w2):
    assert x.shape == (BATCH, SEQ, D_MODEL) and x.dtype == F32
    cos, sin = _rope_tables()
    xt = x.reshape(BATCH * SEQ, D_MODEL)
    for i in range(DEPTH):
        j = i // 2
        if i % 2 == 0:
            x3 = _conv_mixer(xt.reshape(BATCH, SEQ, D_MODEL), mixer_norm[i], conv_w_in[j].astype(BF16),
                             conv_b_in[j], conv_w_dw[j], conv_b_dw[j], conv_ln_g[j], conv_ln_b[j],
                             conv_w_out[j].astype(BF16), conv_b_out[j])
            xt = x3.reshape(BATCH * SEQ, D_MODEL)
        else:
            xt = _attention_layer(xt, mixer_norm[i], attn_w_in[j], attn_q_norm[j], attn_k_norm[j],
                                  attn_w_out[j], cos, sin)
        xt = _mlp(xt, mlp_norm[i], mlp_w1[i].astype(BF16), mlp_w2[i].astype(BF16))
    return xt.reshape(BATCH, SEQ, D_MODEL)
```

```python
import functools

import jax
import jax.numpy as jnp
from jax import lax
from jax.experimental import pallas as pl
from jax.experimental.pallas import tpu as pltpu

D_MODEL = 1024
BATCH = 8
SEQ = 4096
DEPTH = 4
CONV_KERNEL = 31
ATTN_GROUPS = ((128, 1), (512, 4), (2048, 16))
N_GROUPS = len(ATTN_GROUPS)
HEADS = 8
HEAD_DIM = 64
HALF = HEAD_DIM // 2
ATTN_WIDTH = HEADS * HEAD_DIM
D_FF = 4 * D_MODEL
ROPE_THETA = 10000.0
EPS = 1e-6

LANES = 128
WIN = 128
V7X_VMEM_LIMIT = 56 * 1024 * 1024

ROW_TILE = 512
CONV_HALO = 32
CONV_CHUNK = 64
N_SLABS = D_MODEL // LANES
CONV_SLABS = N_SLABS
QKV_CHUNK = 512
ATTN_STEP_TOKENS = 1024
FF_CHUNK = 1024

F32 = jnp.float32
BF16 = jnp.bfloat16
NEG = -1e30
LN2 = 0.6931471805599453
LOG2E = 1.4426950408889634


def _params(n_axes):
    return pltpu.CompilerParams(
        dimension_semantics=("arbitrary",) * n_axes,
        vmem_limit_bytes=V7X_VMEM_LIMIT,
    )


def _const_spec(shape):
    nd = len(shape)
    return pl.BlockSpec(shape, lambda *_: (0,) * nd, pipeline_mode=pl.Buffered(1))


def _rms_rows(x, g):
    return x * lax.rsqrt(jnp.mean(x * x, axis=-1, keepdims=True) + EPS) * g


def _sigmoid(x):
    return 1.0 / (1.0 + jnp.exp(-x))


def _mlp_rows(x, g_ref, w1_ref, w2_ref):
    h = _rms_rows(x, g_ref[...]).astype(BF16)
    acc = x
    for c in range(D_FF // FF_CHUNK):
        cs = slice(c * FF_CHUNK, (c + 1) * FF_CHUNK)
        z = jnp.dot(h, w1_ref[:, cs], preferred_element_type=F32)
        z = jnp.maximum(z, 0.0)
        acc = acc + jnp.dot((z * z).astype(BF16), w2_ref[cs, :], preferred_element_type=F32)
    return acc


def _mlp_specs():
    return [_const_spec((1, D_MODEL)), _const_spec((D_MODEL, D_FF)), _const_spec((D_FF, D_MODEL))]


def _conv_kernel(x_ref, g_ref, win_ref, bin_ref, wdw_ref, bdw_ref, lng_ref, lnb_ref, wout_ref, bout_ref,
                 mg_ref, w1_ref, w2_ref, o_ref, ubuf, cbuf):
    x = x_ref[...]
    h = _rms_rows(x, g_ref[...]).astype(BF16)
    a = jnp.dot(h, win_ref[...], preferred_element_type=F32) + bin_ref[...]
    u = a[:, :D_MODEL] * _sigmoid(a[:, D_MODEL:])

    @pl.when(pl.program_id(1) == 0)
    def _():
        ubuf[:, 0:CONV_HALO, :] = jnp.zeros((CONV_SLABS, CONV_HALO, LANES), F32)

    for c in range(CONV_SLABS):
        ubuf[c, CONV_HALO:CONV_HALO + ROW_TILE, :] = u[:, c * LANES:(c + 1) * LANES]
    first_tap = CONV_HALO - (CONV_KERNEL - 1)

    def slab(c, carry):
        for base in range(0, ROW_TILE, CONV_CHUNK):
            acc = jnp.broadcast_to(bdw_ref[c], (CONV_CHUNK, LANES))
            for k in range(CONV_KERNEL):
                lo = base + first_tap + k
                acc = acc + wdw_ref[c, k:k + 1, :] * ubuf[c, lo:lo + CONV_CHUNK, :]
            cbuf[c, base:base + CONV_CHUNK, :] = acc
        return carry

    lax.fori_loop(0, CONV_SLABS, slab, 0)
    for c in range(CONV_SLABS):
        ubuf[c, 0:CONV_HALO, :] = ubuf[c, ROW_TILE:ROW_TILE + CONV_HALO, :]
    conv = jnp.concatenate([cbuf[c] for c in range(CONV_SLABS)], axis=1)
    mu = jnp.mean(conv, axis=-1, keepdims=True)
    xc = conv - mu
    y = xc * lax.rsqrt(jnp.mean(xc * xc, axis=-1, keepdims=True) + EPS) * lng_ref[...] + lnb_ref[...]
    y = (y * _sigmoid(y)).astype(BF16)
    x_mid = x + jnp.dot(y, wout_ref[...], preferred_element_type=F32) + bout_ref[...]
    o_ref[...] = _mlp_rows(x_mid, mg_ref, w1_ref, w2_ref)


def _conv_layer(x3, g, w_in, b_in, w_dw, b_dw, ln_g, ln_b, w_out, b_out, mlp_g, w1, w2):
    row = pl.BlockSpec((None, ROW_TILE, D_MODEL), lambda b, j: (b, j, 0))
    vec = lambda v: v.reshape(1, -1)
    w_slabs = w_dw.reshape(CONV_KERNEL, CONV_SLABS, LANES).transpose(1, 0, 2)
    b_slabs = b_dw.reshape(CONV_SLABS, 1, LANES)
    return pl.pallas_call(
        _conv_kernel,
        grid=(BATCH, SEQ // ROW_TILE),
        in_specs=[row, _const_spec((1, D_MODEL)), _const_spec((D_MODEL, 2 * D_MODEL)),
                  _const_spec((1, 2 * D_MODEL)), _const_spec((CONV_SLABS, CONV_KERNEL, LANES)),
                  _const_spec((CONV_SLABS, 1, LANES)), _const_spec((1, D_MODEL)), _const_spec((1, D_MODEL)),
                  _const_spec((D_MODEL, D_MODEL)), _const_spec((1, D_MODEL))] + _mlp_specs(),
        out_specs=row,
        out_shape=jax.ShapeDtypeStruct(x3.shape, F32),
        scratch_shapes=[pltpu.VMEM((CONV_SLABS, ROW_TILE + CONV_HALO, LANES), F32),
                        pltpu.VMEM((CONV_SLABS, ROW_TILE, LANES), F32)],
        compiler_params=_params(2),
        name="conv_layer",
    )(x3, vec(g), w_in, vec(b_in), w_slabs, b_slabs, vec(ln_g), vec(ln_b), w_out, vec(b_out), vec(mlp_g), w1, w2)


def _group_tiles(dilation):
    n_sub = SEQ // dilation
    tl = min(n_sub, ATTN_STEP_TOKENS)
    return ATTN_STEP_TOKENS // tl, tl


def _qkv_step_tokens(dilation):
    return max(2 * QKV_CHUNK, WIN * dilation)


def _qkv_kernel(*refs, dilation, tt):
    x_refs = refs[:N_SLABS]
    g_ref, wt_ref, qg_ref, kg_ref, cos_ref, sin_ref, qt_ref, k_ref, vt_ref, rs_scr = refs[N_SLABS:N_SLABS + 10]
    nl = tt // dilation
    ssq = x_refs[0][...] * x_refs[0][...]
    for c in range(1, N_SLABS):
        ssq = ssq + x_refs[c][...] * x_refs[c][...]
    rs = lax.rsqrt(jnp.sum(ssq, axis=-1, keepdims=True) * (1.0 / D_MODEL) + EPS)
    rs_scr[...] = jnp.broadcast_to(rs, (tt, LANES))

    if dilation == 16:
        a_scr, rsa_scr = refs[N_SLABS + 10:]
        quarter = tt // 4
        for q in range(4):
            dst = slice(q * quarter, (q + 1) * quarter)
            for c in range(N_SLABS):
                a_scr[c, dst, :] = x_refs[c][pl.ds(q, quarter, stride=4), :]
            rsa_scr[dst, :] = rs_scr[pl.ds(q, quarter, stride=4), :]

    def residue_rows(r, lo, n):
        if dilation == 1:
            idx = pl.ds(lo, n)
            return [x_refs[c][idx, :] for c in range(N_SLABS)], rs_scr[idx, :]
        if dilation == 4:
            idx = pl.ds(r + lo * 4, n, stride=4)
            return [x_refs[c][idx, :] for c in range(N_SLABS)], rs_scr[idx, :]
        idx = pl.ds((r % 4) * (tt // 4) + r // 4 + lo * 4, n, stride=4)
        return [a_scr[c, idx, :] for c in range(N_SLABS)], rsa_scr[idx, :]

    widen = lambda t: jnp.concatenate([t] * (QKV_CHUNK // LANES), axis=1)
    gains = (widen(qg_ref[...]), widen(kg_ref[...]))
    for ck in range(tt // QKV_CHUNK):
        if nl >= QKV_CHUNK:
            per = nl // QKV_CHUNK
            pieces = [(ck // per, (ck % per) * QKV_CHUNK, QKV_CHUNK)]
        else:
            pieces = [(ck * (QKV_CHUNK // nl) + i, 0, nl) for i in range(QKV_CHUNK // nl)]
        hs = []
        for r, lo, n in pieces:
            slabs, scale = residue_rows(r, lo, n)
            hs.append(jnp.concatenate(
                [slabs[c] * scale * g_ref[:, c * LANES:(c + 1) * LANES] for c in range(N_SLABS)],
                axis=1).astype(BF16))
        h = jnp.concatenate(hs, axis=0) if len(hs) > 1 else hs[0]
        ys = [lax.dot_general(wt_ref[s * ATTN_WIDTH:(s + 1) * ATTN_WIDTH, :], h, (((1,), (1,)), ((), ())),
                              preferred_element_type=F32) for s in range(3)]
        cols = slice(ck * QKV_CHUNK, (ck + 1) * QKV_CHUNK)
        cos = cos_ref[:, cols]
        sin = sin_ref[:, cols]
        rot = []
        for s in range(2):
            parts = []
            for hh in range(HEADS):
                t = ys[s][hh * HEAD_DIM:(hh + 1) * HEAD_DIM, :]
                inv = lax.rsqrt(jnp.mean(t * t, axis=0, keepdims=True) + EPS)
                tn = t * inv * gains[s]
                t1, t2 = tn[:HALF], tn[HALF:]
                parts += [t1 * cos - t2 * sin, t2 * cos + t1 * sin]
            rot.append(jnp.concatenate(parts, axis=0))
        q_t = rot[0].astype(BF16)
        k_rows = rot[1].T.astype(BF16)
        v_t = ys[2].astype(BF16)
        off = 0
        for r, lo, n in pieces:
            qt_ref[r, :, lo:lo + n] = q_t[:, off:off + n]
            k_ref[r, lo:lo + n, :] = k_rows[off:off + n, :]
            vt_ref[r, :, lo:lo + n] = v_t[:, off:off + n]
            off += n


def _qkv_project(x3, g, wt, qg, kg, cos_t, sin_t, dilation):
    tt = _qkv_step_tokens(dilation)
    n_sub = SEQ // dilation
    nl = tt // dilation
    slab = lambda c: pl.BlockSpec((None, tt, LANES), lambda b, j: (b, j, c))
    t_spec = pl.BlockSpec((None, dilation, ATTN_WIDTH, nl), lambda b, j: (b, 0, 0, j))
    r_spec = pl.BlockSpec((None, dilation, nl, ATTN_WIDTH), lambda b, j: (b, 0, j, 0))
    tab = pl.BlockSpec((None, HALF, tt), lambda b, j: (j, 0, 0))
    scratch = [pltpu.VMEM((tt, LANES), F32)]
    if dilation == 16:
        scratch += [pltpu.VMEM((N_SLABS, tt, LANES), F32), pltpu.VMEM((tt, LANES), F32)]
    return pl.pallas_call(
        functools.partial(_qkv_kernel, dilation=dilation, tt=tt),
        grid=(BATCH, SEQ // tt),
        in_specs=[slab(c) for c in range(N_SLABS)]
                 + [_const_spec((1, D_MODEL)), _const_spec((3 * ATTN_WIDTH, D_MODEL)),
                    _const_spec((HEAD_DIM, LANES)), _const_spec((HEAD_DIM, LANES)), tab, tab],
        out_specs=[t_spec, r_spec, t_spec],
        out_shape=[jax.ShapeDtypeStruct((BATCH, dilation, ATTN_WIDTH, n_sub), BF16),
                   jax.ShapeDtypeStruct((BATCH, dilation, n_sub, ATTN_WIDTH), BF16),
                   jax.ShapeDtypeStruct((BATCH, dilation, ATTN_WIDTH, n_sub), BF16)],
        scratch_shapes=scratch,
        compiler_params=_params(2),
        name=f"qkv_d{dilation}",
    )(*([x3] * N_SLABS), g.reshape(1, D_MODEL), wt, qg, kg, cos_t, sin_t)


def _attn_kernel(qt_ref, kc_ref, kp_ref, vc_ref, vp_ref, o_ref, lse_ref, s_scr, p_scr, *, rb_count, qb):
    kj = lax.broadcasted_iota(jnp.int32, (2 * WIN, WIN), 0)
    qi = lax.broadcasted_iota(jnp.int32, (2 * WIN, WIN), 1)
    band = (kj >= qi) & (kj <= qi + WIN)
    bias_rest = jnp.where(band, 0.0, NEG).astype(F32)
    bias_first = jnp.where(band & (kj >= WIN), 0.0, NEG).astype(F32)
    bias_head = jnp.where(pl.program_id(2) > 0, bias_rest, bias_first)
    zeros = jnp.zeros((HEAD_DIM, WIN), BF16)
    pad = jnp.zeros((LANES - HEADS, WIN), F32)
    units = [(rb, i, p) for rb in range(rb_count) for i in range(qb // WIN) for p in range(HEADS // 2)]

    def slices(i, p):
        return (slice(i * WIN, (i + 1) * WIN), slice((i - 1) * WIN, i * WIN),
                slice(p * 2 * HEAD_DIM, (p + 1) * 2 * HEAD_DIM))

    for n, (rb, i, p) in enumerate(units):
        cur, prev, cols = slices(i, p)
        k_prev = kp_ref[rb, :, cols] if i == 0 else kc_ref[rb, prev, cols]
        kk = jnp.concatenate([k_prev, kc_ref[rb, cur, cols]], axis=0)
        q2 = qt_ref[rb, cols, cur]
        qbd = jnp.concatenate(
            [jnp.concatenate([q2[:HEAD_DIM], zeros], axis=1),
             jnp.concatenate([zeros, q2[HEAD_DIM:]], axis=1)], axis=0)
        bias = bias_head if i == 0 else bias_rest
        s_scr[n] = (jnp.dot(kk, qbd, preferred_element_type=F32)
                    + jnp.concatenate([bias, bias], axis=1))

    stats = []
    for n in range(len(units)):
        s = s_scr[n]
        m = jnp.max(s, axis=0, keepdims=True)
        e = jnp.exp2(s - m)
        p_scr[n] = e.astype(BF16)
        stats.append((m, jnp.sum(e, axis=0, keepdims=True)))

    outs, lses = {}, {}
    for n, (rb, i, p) in enumerate(units):
        cur, prev, cols = slices(i, p)
        v_prev = vp_ref[rb, cols, :] if i == 0 else vc_ref[rb, cols, prev]
        vv = jnp.concatenate([v_prev, vc_ref[rb, cols, cur]], axis=1)
        o2 = jnp.dot(vv, p_scr[n], preferred_element_type=F32)
        m, den = stats[n]
        inv = 1.0 / den
        outs.setdefault((rb, i), []).extend(
            [o2[:HEAD_DIM, :WIN] * inv[:, :WIN], o2[HEAD_DIM:, WIN:] * inv[:, WIN:]])
        lse = m * LN2 + jnp.log(den)
        lses.setdefault((rb, i), []).extend([lse[:, :WIN], lse[:, WIN:]])
    for (rb, i), pieces in outs.items():
        cur = slice(i * WIN, (i + 1) * WIN)
        o_ref[cur, rb * ATTN_WIDTH:(rb + 1) * ATTN_WIDTH] = jnp.concatenate(pieces, axis=0).T
        lse_ref[cur, rb * LANES:(rb + 1) * LANES] = jnp.concatenate(lses[(rb, i)] + [pad], axis=0).T


def _attention(qt, k, vt, dilation):
    rb, qb = _group_tiles(dilation)
    n_sub = SEQ // dilation
    per = qb // WIN
    prev_blk = lambda j: jnp.maximum(j * per - 1, 0)
    out, lse = pl.pallas_call(
        functools.partial(_attn_kernel, rb_count=rb, qb=qb),
        grid=(BATCH, dilation // rb, n_sub // qb),
        in_specs=[pl.BlockSpec((None, rb, ATTN_WIDTH, qb), lambda b, r, j: (b, r, 0, j)),
                  pl.BlockSpec((None, rb, qb, ATTN_WIDTH), lambda b, r, j: (b, r, j, 0)),
                  pl.BlockSpec((None, rb, WIN, ATTN_WIDTH), lambda b, r, j: (b, r, prev_blk(j), 0)),
                  pl.BlockSpec((None, rb, ATTN_WIDTH, qb), lambda b, r, j: (b, r, 0, j)),
                  pl.BlockSpec((None, rb, ATTN_WIDTH, WIN), lambda b, r, j: (b, r, 0, prev_blk(j)))],
        out_specs=[pl.BlockSpec((None, qb, rb * ATTN_WIDTH), lambda b, r, j: (b, j, r)),
                   pl.BlockSpec((None, qb, rb * LANES), lambda b, r, j: (b, j, r))],
        out_shape=[jax.ShapeDtypeStruct((BATCH, n_sub, dilation * ATTN_WIDTH), F32),
                   jax.ShapeDtypeStruct((BATCH, n_sub, dilation * LANES), F32)],
        scratch_shapes=[pltpu.VMEM((rb * per * (HEADS // 2), 2 * WIN, 2 * WIN), F32),
                        pltpu.VMEM((rb * per * (HEADS // 2), 2 * WIN, 2 * WIN), BF16)],
        compiler_params=_params(3),
        name=f"attn_d{dilation}",
    )(qt, k, k, vt, vt)
    return out.reshape(BATCH * n_sub, dilation * ATTN_WIDTH), lse.reshape(BATCH * n_sub, dilation * LANES)


def _combine_kernel(x_ref, o0_ref, o1_ref, o2_ref, l0_ref, l1_ref, l2_ref, ex_ref, w_ref, mg_ref, w1_ref, w2_ref,
                    out_ref, o_scr, l_scr):
    n_slab = ATTN_WIDTH // LANES
    for gi, (o_ref, l_ref) in enumerate(((o1_ref, l1_ref), (o2_ref, l2_ref))):
        d = ATTN_GROUPS[gi + 1][1]
        n = ROW_TILE // d
        for r in range(d):
            rows = pl.ds(r, n, stride=d)
            for c in range(n_slab):
                o_scr[gi, c, rows, :] = o_ref[:, r * ATTN_WIDTH + c * LANES: r * ATTN_WIDTH + (c + 1) * LANES]
            l_scr[gi, rows, :] = l_ref[:, r * LANES:(r + 1) * LANES]
    os = [o0_ref[...]] + [jnp.concatenate([o_scr[gi, c] for c in range(n_slab)], axis=1) for gi in range(2)]
    ls = [l0_ref[...], l_scr[0], l_scr[1]]
    m = jnp.maximum(jnp.maximum(ls[0], ls[1]), ls[2])
    es = [jnp.exp(l - m) for l in ls]
    inv = 1.0 / (es[0] + es[1] + es[2])
    mixed = jnp.zeros((ROW_TILE, ATTN_WIDTH), F32)
    for e, o in zip(es, os):
        w = e * inv
        hi = w.astype(BF16)
        lo = (w - hi.astype(F32)).astype(BF16)
        wide = (jnp.dot(hi, ex_ref[...], preferred_element_type=F32)
                + jnp.dot(lo, ex_ref[...], preferred_element_type=F32))
        mixed = mixed + wide * o
    x_mid = x_ref[...] + jnp.dot(mixed.astype(BF16), w_ref[...], preferred_element_type=F32)
    out_ref[...] = _mlp_rows(x_mid, mg_ref, w1_ref, w2_ref)


def _combine_project_mlp(xt, outs, lses, expand, w_out, mlp_g, w1, w2):
    t = xt.shape[0]
    assert ATTN_GROUPS[0][1] == 1
    dil = lambda w: [pl.BlockSpec((ROW_TILE // d, d * w), lambda i: (i, 0)) for _, d in ATTN_GROUPS]
    row = pl.BlockSpec((ROW_TILE, D_MODEL), lambda i: (i, 0))
    return pl.pallas_call(
        _combine_kernel,
        grid=(t // ROW_TILE,),
        in_specs=[row] + dil(ATTN_WIDTH) + dil(LANES)
                 + [_const_spec((LANES, ATTN_WIDTH)), _const_spec((ATTN_WIDTH, D_MODEL))] + _mlp_specs(),
        out_specs=row,
        out_shape=jax.ShapeDtypeStruct(xt.shape, F32),
        scratch_shapes=[pltpu.VMEM((N_GROUPS - 1, ATTN_WIDTH // LANES, ROW_TILE, LANES), F32),
                        pltpu.VMEM((N_GROUPS - 1, ROW_TILE, LANES), F32)],
        compiler_params=_params(1),
        name="attn_combine_mlp",
    )(xt, *outs, *lses, expand, w_out, mlp_g.reshape(1, D_MODEL), w1, w2)


def _rope_tables():
    pos = jnp.arange(SEQ, dtype=F32)
    inv_freq = ROPE_THETA ** (-jnp.arange(0, HEAD_DIM, 2, dtype=F32) / HEAD_DIM)
    ang = pos[:, None] * inv_freq[None, :]
    return jnp.cos(ang), jnp.sin(ang)


def _dilated_table(tab, dilation):
    tt = _qkv_step_tokens(dilation)
    nl = tt // dilation
    return tab.reshape(SEQ // tt, nl, dilation, HALF).transpose(0, 3, 2, 1).reshape(SEQ // tt, HALF, tt)


def _attention_layer(xt, g, w_in, q_gain, k_gain, w_out, cos, sin, mlp_g, w1, w2):
    x3 = xt.reshape(BATCH, SEQ, D_MODEL)
    w4 = w_in.reshape(D_MODEL, 3, N_GROUPS, ATTN_WIDTH)
    lane_bcast = lambda v: jnp.broadcast_to(v.reshape(HEAD_DIM, 1), (HEAD_DIM, LANES))
    qg = lane_bcast(q_gain * (HEAD_DIM ** -0.5 * LOG2E))
    kg = lane_bcast(k_gain)
    head = jnp.arange(LANES)[:, None]
    lane = jnp.arange(ATTN_WIDTH)[None, :]
    expand = (lane // HEAD_DIM == head).astype(BF16)
    outs, lses = [], []
    for gi, (_, dilation) in enumerate(ATTN_GROUPS):
        wt = w4[:, :, gi, :].reshape(D_MODEL, 3 * ATTN_WIDTH).T.astype(BF16)
        qt, k, vt = _qkv_project(x3, g, wt, qg, kg, _dilated_table(cos, dilation),
                                 _dilated_table(sin, dilation), dilation)
        o, l = _attention(qt, k, vt, dilation)
        outs.append(o)
        lses.append(l)
    return _combine_project_mlp(xt, outs, lses, expand, w_out.astype(BF16), mlp_g, w1, w2)


def kernel(x, mixer_norm, mlp_norm, conv_w_in, conv_b_in, conv_w_dw, conv_b_dw, conv_ln_g, conv_ln_b,
           conv_w_out, conv_b_out, attn_w_in, attn_q_norm, attn_k_norm, attn_w_out, mlp_w1, mlp_w2):
    assert x.shape == (BATCH, SEQ, D_MODEL) and x.dtype == F32
    cos, sin = _rope_tables()
    xt = x.reshape(BATCH * SEQ, D_MODEL)
    for i in range(DEPTH):
        j = i // 2
        mlp = (mlp_norm[i], mlp_w1[i].astype(BF16), mlp_w2[i].astype(BF16))
        if i % 2 == 0:
            x3 = _conv_layer(xt.reshape(BATCH, SEQ, D_MODEL), mixer_norm[i], conv_w_in[j].astype(BF16),
                             conv_b_in[j], conv_w_dw[j], conv_b_dw[j], conv_ln_g[j], conv_ln_b[j],
                             conv_w_out[j].astype(BF16), conv_b_out[j], *mlp)
            xt = x3.reshape(BATCH * SEQ, D_MODEL)
        else:
            xt = _attention_layer(xt, mixer_norm[i], attn_w_in[j], attn_q_norm[j], attn_k_norm[j],
                                  attn_w_out[j], cos, sin, *mlp)
    return xt.reshape(BATCH, SEQ, D_MODEL)
```

```python
import functools

import jax
import jax.numpy as jnp
from jax import lax
from jax.experimental import pallas as pl
from jax.experimental.pallas import tpu as pltpu

D_MODEL = 1024
BATCH = 8
SEQ = 4096
DEPTH = 4
CONV_KERNEL = 31
ATTN_GROUPS = ((128, 1), (512, 4), (2048, 16))
N_GROUPS = len(ATTN_GROUPS)
HEADS = 8
HEAD_DIM = 64
HALF = HEAD_DIM // 2
ATTN_WIDTH = HEADS * HEAD_DIM
D_FF = 4 * D_MODEL
ROPE_THETA = 10000.0
EPS = 1e-6

LANES = 128
WIN = 128
V7X_VMEM_LIMIT = 56 * 1024 * 1024

ROW_TILE = 512
CONV_HALO = 32
CONV_CHUNK = 64
CONV_PAIR = CONV_CHUNK // 2
CONV_GROUP = 4
N_SLABS = D_MODEL // LANES
CONV_SLABS = N_SLABS
QKV_CHUNK = 512
ATTN_STEP_TOKENS = 1024
FF_CHUNK = 1024

F32 = jnp.float32
BF16 = jnp.bfloat16
NEG = -1e30
LN2 = 0.6931471805599453
LOG2E = 1.4426950408889634


def _params(n_axes):
    return pltpu.CompilerParams(
        dimension_semantics=("arbitrary",) * n_axes,
        vmem_limit_bytes=V7X_VMEM_LIMIT,
    )


def _const_spec(shape):
    nd = len(shape)
    return pl.BlockSpec(shape, lambda *_: (0,) * nd, pipeline_mode=pl.Buffered(1))


def _rms_rows(x, g):
    return x * lax.rsqrt(jnp.mean(x * x, axis=-1, keepdims=True) + EPS) * g


def _sigmoid(x):
    return 1.0 / (1.0 + jnp.exp(-x))


def _mlp_rows(x, g_ref, w1_ref, w2_ref):
    h = _rms_rows(x, g_ref[...]).astype(BF16)
    acc = x
    for c in range(D_FF // FF_CHUNK):
        cs = slice(c * FF_CHUNK, (c + 1) * FF_CHUNK)
        z = jnp.dot(h, w1_ref[:, cs], preferred_element_type=F32)
        z = jnp.maximum(z, 0.0)
        acc = acc + jnp.dot((z * z).astype(BF16), w2_ref[cs, :], preferred_element_type=F32)
    return acc


def _mlp_specs():
    return [_const_spec((1, D_MODEL)), _const_spec((D_MODEL, D_FF)), _const_spec((D_FF, D_MODEL))]


def _conv_kernel(x_ref, g_ref, win_ref, bin_ref, wdw_ref, bdw_ref, lng_ref, lnb_ref, wout_ref, bout_ref,
                 mg_ref, w1_ref, w2_ref, o_ref, ubuf, pkbuf, cbuf):
    x = x_ref[...]
    h = _rms_rows(x, g_ref[...]).astype(BF16)
    a = jnp.dot(h, win_ref[...], preferred_element_type=F32) + bin_ref[...]
    u = (a[:, :D_MODEL] * _sigmoid(a[:, D_MODEL:])).astype(BF16).astype(F32)

    @pl.when(pl.program_id(1) == 0)
    def _():
        ubuf[:, 0:CONV_HALO, :] = jnp.zeros((CONV_SLABS, CONV_HALO, LANES), F32)

    n_words = CONV_HALO + ROW_TILE - CONV_PAIR
    for c in range(CONV_SLABS):
        ubuf[c, CONV_HALO:CONV_HALO + ROW_TILE, :] = u[:, c * LANES:(c + 1) * LANES]
        bits = pltpu.bitcast(ubuf[c], jnp.uint32)
        pkbuf[c] = lax.shift_right_logical(bits[:n_words], jnp.uint32(16)) | bits[CONV_PAIR:CONV_PAIR + n_words]
    first_tap = CONV_HALO - (CONV_KERNEL - 1)
    high_half = jnp.uint32(0xFFFF0000)

    def slab(c, carry):
        for base in range(0, ROW_TILE, CONV_CHUNK):
            acc_lo = jnp.broadcast_to(bdw_ref[c], (CONV_PAIR, LANES))
            acc_hi = acc_lo
            for k0 in range(0, CONV_KERNEL, CONV_GROUP):
                part = None
                for k in range(k0, min(k0 + CONV_GROUP, CONV_KERNEL)):
                    lo = base + first_tap + k
                    win = pltpu.bitcast(pkbuf[c, lo:lo + CONV_PAIR, :], BF16)
                    wk = jnp.concatenate([wdw_ref[c, k]] * (CONV_CHUNK // 16), axis=0)
                    part = win * wk if part is None else part + win * wk
                words = pltpu.bitcast(part, jnp.uint32)
                acc_lo = acc_lo + pltpu.bitcast(lax.shift_left(words, jnp.uint32(16)), F32)
                acc_hi = acc_hi + pltpu.bitcast(words & high_half, F32)
            cbuf[c, base:base + CONV_PAIR, :] = acc_lo
            cbuf[c, base + CONV_PAIR:base + CONV_CHUNK, :] = acc_hi
        return carry

    lax.fori_loop(0, CONV_SLABS, slab, 0)
    for c in range(CONV_SLABS):
        ubuf[c, 0:CONV_HALO, :] = ubuf[c, ROW_TILE:ROW_TILE + CONV_HALO, :]
    conv = jnp.concatenate([cbuf[c] for c in range(CONV_SLABS)], axis=1)
    mu = jnp.mean(conv, axis=-1, keepdims=True)
    xc = conv - mu
    y = xc * lax.rsqrt(jnp.mean(xc * xc, axis=-1, keepdims=True) + EPS) * lng_ref[...] + lnb_ref[...]
    y = (y * _sigmoid(y)).astype(BF16)
    x_mid = x + jnp.dot(y, wout_ref[...], preferred_element_type=F32) + bout_ref[...]
    o_ref[...] = _mlp_rows(x_mid, mg_ref, w1_ref, w2_ref)


def _conv_layer(x3, g, w_in, b_in, w_dw, b_dw, ln_g, ln_b, w_out, b_out, mlp_g, w1, w2):
    row = pl.BlockSpec((None, ROW_TILE, D_MODEL), lambda b, j: (b, j, 0))
    vec = lambda v: v.reshape(1, -1)
    w_slabs = jnp.broadcast_to(w_dw.reshape(CONV_KERNEL, CONV_SLABS, 1, LANES).transpose(1, 0, 2, 3),
                               (CONV_SLABS, CONV_KERNEL, 16, LANES)).astype(BF16)
    b_slabs = b_dw.reshape(CONV_SLABS, 1, LANES)
    return pl.pallas_call(
        _conv_kernel,
        grid=(BATCH, SEQ // ROW_TILE),
        in_specs=[row, _const_spec((1, D_MODEL)), _const_spec((D_MODEL, 2 * D_MODEL)),
                  _const_spec((1, 2 * D_MODEL)), _const_spec((CONV_SLABS, CONV_KERNEL, 16, LANES)),
                  _const_spec((CONV_SLABS, 1, LANES)), _const_spec((1, D_MODEL)), _const_spec((1, D_MODEL)),
                  _const_spec((D_MODEL, D_MODEL)), _const_spec((1, D_MODEL))] + _mlp_specs(),
        out_specs=row,
        out_shape=jax.ShapeDtypeStruct(x3.shape, F32),
        scratch_shapes=[pltpu.VMEM((CONV_SLABS, ROW_TILE + CONV_HALO, LANES), F32),
                        pltpu.VMEM((CONV_SLABS, ROW_TILE + CONV_HALO - CONV_PAIR, LANES), jnp.uint32),
                        pltpu.VMEM((CONV_SLABS, ROW_TILE, LANES), F32)],
        compiler_params=_params(2),
        name="conv_layer",
    )(x3, vec(g), w_in, vec(b_in), w_slabs, b_slabs, vec(ln_g), vec(ln_b), w_out, vec(b_out), vec(mlp_g), w1, w2)


def _group_tiles(dilation):
    n_sub = SEQ // dilation
    tl = min(n_sub, ATTN_STEP_TOKENS)
    return ATTN_STEP_TOKENS // tl, tl


def _qkv_step_tokens(dilation):
    return max(4 * QKV_CHUNK, WIN * dilation)


def _qkv_kernel(*refs, dilation, tt):
    x_refs = refs[:N_SLABS]
    w_refs = refs[N_SLABS:N_SLABS + 3]
    qg_ref, kg_ref, cos_ref, sin_ref, qt_ref, k_ref, vt_ref, rs_scr = refs[N_SLABS + 3:N_SLABS + 11]
    nl = tt // dilation
    ssq = x_refs[0][...] * x_refs[0][...]
    for c in range(1, N_SLABS):
        ssq = ssq + x_refs[c][...] * x_refs[c][...]
    rs = lax.rsqrt(jnp.sum(ssq, axis=-1, keepdims=True) * (1.0 / D_MODEL) + EPS)
    rs_scr[...] = jnp.broadcast_to(rs, (tt, LANES))

    if dilation == 16:
        a_scr, rsa_scr = refs[N_SLABS + 11:]
        quarter = tt // 4
        for q in range(4):
            dst = slice(q * quarter, (q + 1) * quarter)
            for c in range(N_SLABS):
                a_scr[c, dst, :] = x_refs[c][pl.ds(q, quarter, stride=4), :]
            rsa_scr[dst, :] = rs_scr[pl.ds(q, quarter, stride=4), :]

    def residue_rows(r, lo, n):
        if dilation == 1:
            idx = pl.ds(lo, n)
            return [x_refs[c][idx, :] for c in range(N_SLABS)], rs_scr[idx, :]
        if dilation == 4:
            idx = pl.ds(r + lo * 4, n, stride=4)
            return [x_refs[c][idx, :] for c in range(N_SLABS)], rs_scr[idx, :]
        idx = pl.ds((r % 4) * (tt // 4) + r // 4 + lo * 4, n, stride=4)
        return [a_scr[c, idx, :] for c in range(N_SLABS)], rsa_scr[idx, :]

    widen = lambda t: jnp.concatenate([t] * (QKV_CHUNK // LANES), axis=1)
    gains = (widen(qg_ref[...]), widen(kg_ref[...]))
    for ck in range(tt // QKV_CHUNK):
        if nl >= QKV_CHUNK:
            per = nl // QKV_CHUNK
            pieces = [(ck // per, (ck % per) * QKV_CHUNK, QKV_CHUNK)]
        else:
            pieces = [(ck * (QKV_CHUNK // nl) + i, 0, nl) for i in range(QKV_CHUNK // nl)]
        hs = []
        for r, lo, n in pieces:
            slabs, scale = residue_rows(r, lo, n)
            hs.append(jnp.concatenate([slabs[c] * scale for c in range(N_SLABS)], axis=1).astype(BF16))
        h = jnp.concatenate(hs, axis=0) if len(hs) > 1 else hs[0]
        ys = [lax.dot_general(w_ref[...], h, (((1,), (1,)), ((), ())), preferred_element_type=F32)
              for w_ref in w_refs]
        cols = slice(ck * QKV_CHUNK, (ck + 1) * QKV_CHUNK)
        cos = cos_ref[:, cols]
        sin = sin_ref[:, cols]
        rot = []
        for s in range(2):
            g1, g2 = gains[s][:HALF], gains[s][HALF:]
            c1, s2, c2, s1 = g1 * cos, g2 * sin, g2 * cos, g1 * sin
            parts = []
            for hh in range(HEADS):
                t = ys[s][hh * HEAD_DIM:(hh + 1) * HEAD_DIM, :]
                inv = lax.rsqrt(jnp.mean(t * t, axis=0, keepdims=True) + EPS)
                t1, t2 = t[:HALF], t[HALF:]
                parts += [(t1 * c1 - t2 * s2) * inv, (t2 * c2 + t1 * s1) * inv]
            rot.append(jnp.concatenate(parts, axis=0))
        q_t = rot[0].astype(BF16)
        k_rows = rot[1].T.astype(BF16)
        v_t = ys[2].astype(BF16)
        off = 0
        for r, lo, n in pieces:
            qt_ref[r, :, lo:lo + n] = q_t[:, off:off + n]
            k_ref[r, lo:lo + n, :] = k_rows[off:off + n, :]
            vt_ref[r, :, lo:lo + n] = v_t[:, off:off + n]
            off += n


def _qkv_project(x3, wt_all, group, qg, kg, cos_t, sin_t, dilation):
    tt = _qkv_step_tokens(dilation)
    n_sub = SEQ // dilation
    nl = tt // dilation
    slab = lambda c: pl.BlockSpec((None, tt, LANES), lambda b, j: (b, j, c))
    t_spec = pl.BlockSpec((None, dilation, ATTN_WIDTH, nl), lambda b, j: (b, 0, 0, j))
    r_spec = pl.BlockSpec((None, dilation, nl, ATTN_WIDTH), lambda b, j: (b, 0, j, 0))
    tab = pl.BlockSpec((None, HALF, tt), lambda b, j: (j, 0, 0))
    scratch = [pltpu.VMEM((tt, LANES), F32)]
    if dilation == 16:
        scratch += [pltpu.VMEM((N_SLABS, tt, LANES), F32), pltpu.VMEM((tt, LANES), F32)]
    return pl.pallas_call(
        functools.partial(_qkv_kernel, dilation=dilation, tt=tt),
        grid=(BATCH, SEQ // tt),
        in_specs=[slab(c) for c in range(N_SLABS)]
                 + [pl.BlockSpec((ATTN_WIDTH, D_MODEL), lambda b, j, s=s: (s * N_GROUPS + group, 0),
                                 pipeline_mode=pl.Buffered(1)) for s in range(3)]
                 + [_const_spec((HEAD_DIM, LANES)), _const_spec((HEAD_DIM, LANES)), tab, tab],
        out_specs=[t_spec, r_spec, t_spec],
        out_shape=[jax.ShapeDtypeStruct((BATCH, dilation, ATTN_WIDTH, n_sub), BF16),
                   jax.ShapeDtypeStruct((BATCH, dilation, n_sub, ATTN_WIDTH), BF16),
                   jax.ShapeDtypeStruct((BATCH, dilation, ATTN_WIDTH, n_sub), BF16)],
        scratch_shapes=scratch,
        compiler_params=_params(2),
        name=f"qkv_d{dilation}",
    )(*([x3] * N_SLABS), wt_all, wt_all, wt_all, qg, kg, cos_t, sin_t)


def _attn_kernel(qt_ref, kc_ref, kp_ref, vc_ref, vp_ref, o_ref, lse_ref, s_scr, p_scr, *, rb_count, qb):
    kj = lax.broadcasted_iota(jnp.int32, (2 * WIN, WIN), 0)
    qi = lax.broadcasted_iota(jnp.int32, (2 * WIN, WIN), 1)
    band = (kj >= qi) & (kj <= qi + WIN)
    bias_rest = jnp.where(band, 0.0, NEG).astype(F32)
    bias_first = jnp.where(band & (kj >= WIN), 0.0, NEG).astype(F32)
    bias_head = jnp.where(pl.program_id(2) > 0, bias_rest, bias_first)
    zeros = jnp.zeros((HEAD_DIM, WIN), BF16)
    pad = jnp.zeros((LANES - HEADS, WIN), F32)
    units = [(rb, i, p) for rb in range(rb_count) for i in range(qb // WIN) for p in range(HEADS // 2)]

    def slices(i, p):
        return (slice(i * WIN, (i + 1) * WIN), slice((i - 1) * WIN, i * WIN),
                slice(p * 2 * HEAD_DIM, (p + 1) * 2 * HEAD_DIM))

    for n, (rb, i, p) in enumerate(units):
        cur, prev, cols = slices(i, p)
        k_prev = kp_ref[rb, :, cols] if i == 0 else kc_ref[rb, prev, cols]
        kk = jnp.concatenate([k_prev, kc_ref[rb, cur, cols]], axis=0)
        q2 = qt_ref[rb, cols, cur]
        qbd = jnp.concatenate(
            [jnp.concatenate([q2[:HEAD_DIM], zeros], axis=1),
             jnp.concatenate([zeros, q2[HEAD_DIM:]], axis=1)], axis=0)
        bias = bias_head if i == 0 else bias_rest
        s_scr[n] = (jnp.dot(kk, qbd, preferred_element_type=F32)
                    + jnp.concatenate([bias, bias], axis=1))

    stats = []
    for n in range(len(units)):
        s = s_scr[n]
        m = jnp.max(s, axis=0, keepdims=True)
        e = jnp.exp2(s - m)
        p_scr[n] = e.astype(BF16)
        stats.append((m, jnp.sum(e, axis=0, keepdims=True)))

    outs, lses = {}, {}
    for n, (rb, i, p) in enumerate(units):
        cur, prev, cols = slices(i, p)
        v_prev = vp_ref[rb, cols, :] if i == 0 else vc_ref[rb, cols, prev]
        vv = jnp.concatenate([v_prev, vc_ref[rb, cols, cur]], axis=1)
        o2 = jnp.dot(vv, p_scr[n], preferred_element_type=F32)
        m, den = stats[n]
        inv = 1.0 / den
        outs.setdefault((rb, i), []).extend(
            [o2[:HEAD_DIM, :WIN] * inv[:, :WIN], o2[HEAD_DIM:, WIN:] * inv[:, WIN:]])
        lse = m * LN2 + jnp.log(den)
        lses.setdefault((rb, i), []).extend([lse[:, :WIN], lse[:, WIN:]])
    for (rb, i), pieces in outs.items():
        cur = slice(i * WIN, (i + 1) * WIN)
        o_ref[cur, rb * ATTN_WIDTH:(rb + 1) * ATTN_WIDTH] = jnp.concatenate(pieces, axis=0).T
        lse_ref[cur, rb * LANES:(rb + 1) * LANES] = jnp.concatenate(lses[(rb, i)] + [pad], axis=0).T


def _attention(qt, k, vt, dilation):
    rb, qb = _group_tiles(dilation)
    n_sub = SEQ // dilation
    per = qb // WIN
    prev_blk = lambda j: jnp.maximum(j * per - 1, 0)
    out, lse = pl.pallas_call(
        functools.partial(_attn_kernel, rb_count=rb, qb=qb),
        grid=(BATCH, dilation // rb, n_sub // qb),
        in_specs=[pl.BlockSpec((None, rb, ATTN_WIDTH, qb), lambda b, r, j: (b, r, 0, j)),
                  pl.BlockSpec((None, rb, qb, ATTN_WIDTH), lambda b, r, j: (b, r, j, 0)),
                  pl.BlockSpec((None, rb, WIN, ATTN_WIDTH), lambda b, r, j: (b, r, prev_blk(j), 0)),
                  pl.BlockSpec((None, rb, ATTN_WIDTH, qb), lambda b, r, j: (b, r, 0, j)),
                  pl.BlockSpec((None, rb, ATTN_WIDTH, WIN), lambda b, r, j: (b, r, 0, prev_blk(j)))],
        out_specs=[pl.BlockSpec((None, qb, rb * ATTN_WIDTH), lambda b, r, j: (b, j, r)),
                   pl.BlockSpec((None, qb, rb * LANES), lambda b, r, j: (b, j, r))],
        out_shape=[jax.ShapeDtypeStruct((BATCH, n_sub, dilation * ATTN_WIDTH), F32),
                   jax.ShapeDtypeStruct((BATCH, n_sub, dilation * LANES), F32)],
        scratch_shapes=[pltpu.VMEM((rb * per * (HEADS // 2), 2 * WIN, 2 * WIN), F32),
                        pltpu.VMEM((rb * per * (HEADS // 2), 2 * WIN, 2 * WIN), BF16)],
        compiler_params=_params(3),
        name=f"attn_d{dilation}",
    )(qt, k, k, vt, vt)
    return out.reshape(BATCH * n_sub, dilation * ATTN_WIDTH), lse.reshape(BATCH * n_sub, dilation * LANES)


def _combine_kernel(x_ref, o0_ref, o1_ref, o2_ref, l0_ref, l1_ref, l2_ref, ex_ref, w_ref, mg_ref, w1_ref, w2_ref,
                    out_ref, o_scr, l_scr):
    n_slab = ATTN_WIDTH // LANES
    for gi, (o_ref, l_ref) in enumerate(((o1_ref, l1_ref), (o2_ref, l2_ref))):
        d = ATTN_GROUPS[gi + 1][1]
        n = ROW_TILE // d
        for r in range(d):
            rows = pl.ds(r, n, stride=d)
            for c in range(n_slab):
                o_scr[gi, c, rows, :] = o_ref[:, r * ATTN_WIDTH + c * LANES: r * ATTN_WIDTH + (c + 1) * LANES]
            l_scr[gi, rows, :] = l_ref[:, r * LANES:(r + 1) * LANES]
    os = [o0_ref[...]] + [jnp.concatenate([o_scr[gi, c] for c in range(n_slab)], axis=1) for gi in range(2)]
    ls = [l0_ref[...], l_scr[0], l_scr[1]]
    m = jnp.maximum(jnp.maximum(ls[0], ls[1]), ls[2])
    es = [jnp.exp(l - m) for l in ls]
    inv = 1.0 / (es[0] + es[1] + es[2])
    mixed = jnp.zeros((ROW_TILE, ATTN_WIDTH), F32)
    for e, o in zip(es, os):
        wide = jnp.dot((e * inv).astype(BF16), ex_ref[...], preferred_element_type=F32)
        mixed = mixed + wide * o
    x_mid = x_ref[...] + jnp.dot(mixed.astype(BF16), w_ref[...], preferred_element_type=F32)
    out_ref[...] = _mlp_rows(x_mid, mg_ref, w1_ref, w2_ref)


def _combine_project_mlp(xt, outs, lses, expand, w_out, mlp_g, w1, w2):
    t = xt.shape[0]
    assert ATTN_GROUPS[0][1] == 1
    dil = lambda w: [pl.BlockSpec((ROW_TILE // d, d * w), lambda i: (i, 0)) for _, d in ATTN_GROUPS]
    row = pl.BlockSpec((ROW_TILE, D_MODEL), lambda i: (i, 0))
    return pl.pallas_call(
        _combine_kernel,
        grid=(t // ROW_TILE,),
        in_specs=[row] + dil(ATTN_WIDTH) + dil(LANES)
                 + [_const_spec((LANES, ATTN_WIDTH)), _const_spec((ATTN_WIDTH, D_MODEL))] + _mlp_specs(),
        out_specs=row,
        out_shape=jax.ShapeDtypeStruct(xt.shape, F32),
        scratch_shapes=[pltpu.VMEM((N_GROUPS - 1, ATTN_WIDTH // LANES, ROW_TILE, LANES), F32),
                        pltpu.VMEM((N_GROUPS - 1, ROW_TILE, LANES), F32)],
        compiler_params=_params(1),
        name="attn_combine_mlp",
    )(xt, *outs, *lses, expand, w_out, mlp_g.reshape(1, D_MODEL), w1, w2)


def _rope_tables():
    pos = jnp.arange(SEQ, dtype=F32)
    inv_freq = ROPE_THETA ** (-jnp.arange(0, HEAD_DIM, 2, dtype=F32) / HEAD_DIM)
    ang = pos[:, None] * inv_freq[None, :]
    return jnp.cos(ang), jnp.sin(ang)


def _dilated_table(tab, dilation):
    tt = _qkv_step_tokens(dilation)
    nl = tt // dilation
    return tab.reshape(SEQ // tt, nl, dilation, HALF).transpose(0, 3, 2, 1).reshape(SEQ // tt, HALF, tt)


def _attention_layer(xt, g, w_in, q_gain, k_gain, w_out, cos, sin, mlp_g, w1, w2):
    x3 = xt.reshape(BATCH, SEQ, D_MODEL)
    wt_all = (w_in * g[:, None]).T.astype(BF16)
    lane_bcast = lambda v: jnp.broadcast_to(v.reshape(HEAD_DIM, 1), (HEAD_DIM, LANES))
    qg = lane_bcast(q_gain * (HEAD_DIM ** -0.5 * LOG2E))
    kg = lane_bcast(k_gain)
    head = jnp.arange(LANES)[:, None]
    lane = jnp.arange(ATTN_WIDTH)[None, :]
    expand = (lane // HEAD_DIM == head).astype(BF16)
    outs, lses = [], []
    for gi, (_, dilation) in enumerate(ATTN_GROUPS):
        qt, k, vt = _qkv_project(x3, wt_all, gi, qg, kg, _dilated_table(cos, dilation),
                                 _dilated_table(sin, dilation), dilation)
        o, l = _attention(qt, k, vt, dilation)
        outs.append(o)
        lses.append(l)
    return _combine_project_mlp(xt, outs, lses, expand, w_out.astype(BF16), mlp_g, w1, w2)


def kernel(x, mixer_norm, mlp_norm, conv_w_in, conv_b_in, conv_w_dw, conv_b_dw, conv_ln_g, conv_ln_b,
           conv_w_out, conv_b_out, attn_w_in, attn_q_norm, attn_k_norm, attn_w_out, mlp_w1, mlp_w2):
    assert x.shape == (BATCH, SEQ, D_MODEL) and x.dtype == F32
    cos, sin = _rope_tables()
    xt = x.reshape(BATCH * SEQ, D_MODEL)
    for i in range(DEPTH):
        j = i // 2
        mlp = (mlp_norm[i], mlp_w1[i].astype(BF16), mlp_w2[i].astype(BF16))
        if i % 2 == 0:
            x3 = _conv_layer(xt.reshape(BATCH, SEQ, D_MODEL), mixer_norm[i], conv_w_in[j].astype(BF16),
                             conv_b_in[j], conv_w_dw[j], conv_b_dw[j], conv_ln_g[j], conv_ln_b[j],
                             conv_w_out[j].astype(BF16), conv_b_out[j], *mlp)
            xt = x3.reshape(BATCH * SEQ, D_MODEL)
        else:
            xt = _attention_layer(xt, mixer_norm[i], attn_w_in[j], attn_q_norm[j], attn_k_norm[j],
                                  attn_w_out[j], cos, sin, *mlp)
    return xt.reshape(BATCH, SEQ, D_MODEL)
```

```python
import functools

import jax
import jax.numpy as jnp
from jax import lax
from jax.experimental import pallas as pl
from jax.experimental.pallas import tpu as pltpu

D_MODEL = 1024
BATCH = 8
SEQ = 4096
DEPTH = 4
CONV_KERNEL = 31
ATTN_GROUPS = ((128, 1), (512, 4), (2048, 16))
N_GROUPS = len(ATTN_GROUPS)
HEADS = 8
HEAD_DIM = 64
HALF = HEAD_DIM // 2
ATTN_WIDTH = HEADS * HEAD_DIM
D_FF = 4 * D_MODEL
ROPE_THETA = 10000.0
EPS = 1e-6

LANES = 128
BF16_ROWS = 16
WIN = 128
V7X_VMEM_LIMIT = 56 * 1024 * 1024

ROW_TILE = 512
CONV_HALO = 32
CONV_CHUNK = 64
CONV_PAIR = CONV_CHUNK // 2
CONV_GROUP = 4
N_SLABS = D_MODEL // LANES
CONV_SLABS = N_SLABS
QKV_CHUNK = 512
ATTN_STEP_TOKENS = 2048
FF_CHUNK = 1024

F32 = jnp.float32
BF16 = jnp.bfloat16
NEG = -1e30
LN2 = 0.6931471805599453
LOG2E = 1.4426950408889634


def _params(n_axes):
    return pltpu.CompilerParams(
        dimension_semantics=("arbitrary",) * n_axes,
        vmem_limit_bytes=V7X_VMEM_LIMIT,
    )


def _const_spec(shape):
    nd = len(shape)
    return pl.BlockSpec(shape, lambda *_: (0,) * nd, pipeline_mode=pl.Buffered(1))


def _rms_rows(x, g):
    return x * lax.rsqrt(jnp.mean(x * x, axis=-1, keepdims=True) + EPS) * g


def _sigmoid(x):
    return 1.0 / (1.0 + jnp.exp(-x))


def _mlp_rows(x, g_ref, w1_ref, w2_ref):
    h = _rms_rows(x, g_ref[...]).astype(BF16)
    acc = x
    for c in range(D_FF // FF_CHUNK):
        cs = slice(c * FF_CHUNK, (c + 1) * FF_CHUNK)
        z = jnp.dot(h, w1_ref[:, cs], preferred_element_type=F32)
        z = jnp.maximum(z, 0.0)
        acc = acc + jnp.dot((z * z).astype(BF16), w2_ref[cs, :], preferred_element_type=F32)
    return acc


def _mlp_specs():
    return [_const_spec((1, D_MODEL)), _const_spec((D_MODEL, D_FF)), _const_spec((D_FF, D_MODEL))]


def _conv_kernel(x_ref, g_ref, win_ref, bin_ref, wdw_ref, bdw_ref, lng_ref, lnb_ref, wout_ref, bout_ref,
                 mg_ref, w1_ref, w2_ref, o_ref, ubuf, pkbuf, cbuf):
    x = x_ref[...]
    h = _rms_rows(x, g_ref[...]).astype(BF16)
    a = jnp.dot(h, win_ref[...], preferred_element_type=F32) + bin_ref[...]
    u = (a[:, :D_MODEL] * _sigmoid(a[:, D_MODEL:])).astype(BF16).astype(F32)

    @pl.when(pl.program_id(1) == 0)
    def _():
        ubuf[:, 0:CONV_HALO, :] = jnp.zeros((CONV_SLABS, CONV_HALO, LANES), F32)

    n_words = CONV_HALO + ROW_TILE - CONV_PAIR
    for c in range(CONV_SLABS):
        ubuf[c, CONV_HALO:CONV_HALO + ROW_TILE, :] = u[:, c * LANES:(c + 1) * LANES]
        bits = pltpu.bitcast(ubuf[c], jnp.uint32)
        pkbuf[c] = lax.shift_right_logical(bits[:n_words], jnp.uint32(16)) | bits[CONV_PAIR:CONV_PAIR + n_words]
    first_tap = CONV_HALO - (CONV_KERNEL - 1)
    high_half = jnp.uint32(0xFFFF0000)

    def slab(c, carry):
        for base in range(0, ROW_TILE, CONV_CHUNK):
            acc_lo = jnp.broadcast_to(bdw_ref[c], (CONV_PAIR, LANES))
            acc_hi = acc_lo
            for k0 in range(0, CONV_KERNEL, CONV_GROUP):
                part = None
                for k in range(k0, min(k0 + CONV_GROUP, CONV_KERNEL)):
                    lo = base + first_tap + k
                    win = pltpu.bitcast(pkbuf[c, lo:lo + CONV_PAIR, :], BF16)
                    wk = jnp.concatenate([wdw_ref[c, k]] * (CONV_CHUNK // BF16_ROWS), axis=0)
                    part = win * wk if part is None else part + win * wk
                words = pltpu.bitcast(part, jnp.uint32)
                acc_lo = acc_lo + pltpu.bitcast(lax.shift_left(words, jnp.uint32(16)), F32)
                acc_hi = acc_hi + pltpu.bitcast(words & high_half, F32)
            cbuf[c, base:base + CONV_PAIR, :] = acc_lo
            cbuf[c, base + CONV_PAIR:base + CONV_CHUNK, :] = acc_hi
        return carry

    lax.fori_loop(0, CONV_SLABS, slab, 0)
    for c in range(CONV_SLABS):
        ubuf[c, 0:CONV_HALO, :] = ubuf[c, ROW_TILE:ROW_TILE + CONV_HALO, :]
    conv = jnp.concatenate([cbuf[c] for c in range(CONV_SLABS)], axis=1)
    mu = jnp.mean(conv, axis=-1, keepdims=True)
    xc = conv - mu
    y = xc * lax.rsqrt(jnp.mean(xc * xc, axis=-1, keepdims=True) + EPS) * lng_ref[...] + lnb_ref[...]
    y = (y * _sigmoid(y)).astype(BF16)
    x_mid = x + jnp.dot(y, wout_ref[...], preferred_element_type=F32) + bout_ref[...]
    o_ref[...] = _mlp_rows(x_mid, mg_ref, w1_ref, w2_ref)


def _conv_layer(x3, g, w_in, b_in, w_dw, b_dw, ln_g, ln_b, w_out, b_out, mlp_g, w1, w2):
    row = pl.BlockSpec((None, ROW_TILE, D_MODEL), lambda b, j: (b, j, 0))
    vec = lambda v: v.reshape(1, -1)
    w_slabs = jnp.broadcast_to(w_dw.reshape(CONV_KERNEL, CONV_SLABS, 1, LANES).transpose(1, 0, 2, 3),
                               (CONV_SLABS, CONV_KERNEL, BF16_ROWS, LANES)).astype(BF16)
    b_slabs = b_dw.reshape(CONV_SLABS, 1, LANES)
    return pl.pallas_call(
        _conv_kernel,
        grid=(BATCH, SEQ // ROW_TILE),
        in_specs=[row, _const_spec((1, D_MODEL)), _const_spec((D_MODEL, 2 * D_MODEL)),
                  _const_spec((1, 2 * D_MODEL)), _const_spec((CONV_SLABS, CONV_KERNEL, BF16_ROWS, LANES)),
                  _const_spec((CONV_SLABS, 1, LANES)), _const_spec((1, D_MODEL)), _const_spec((1, D_MODEL)),
                  _const_spec((D_MODEL, D_MODEL)), _const_spec((1, D_MODEL))] + _mlp_specs(),
        out_specs=row,
        out_shape=jax.ShapeDtypeStruct(x3.shape, F32),
        scratch_shapes=[pltpu.VMEM((CONV_SLABS, ROW_TILE + CONV_HALO, LANES), F32),
                        pltpu.VMEM((CONV_SLABS, ROW_TILE + CONV_HALO - CONV_PAIR, LANES), jnp.uint32),
                        pltpu.VMEM((CONV_SLABS, ROW_TILE, LANES), F32)],
        compiler_params=_params(2),
        name="conv_layer",
    )(x3, vec(g), w_in, vec(b_in), w_slabs, b_slabs, vec(ln_g), vec(ln_b), w_out, vec(b_out), vec(mlp_g), w1, w2)


def _group_tiles(dilation):
    n_sub = SEQ // dilation
    tl = min(n_sub, ATTN_STEP_TOKENS)
    return ATTN_STEP_TOKENS // tl, tl


def _qkv_step_tokens(dilation):
    return max(4 * QKV_CHUNK, WIN * dilation)


def _qkv_kernel(*refs, dilation, tt):
    x_refs = refs[:N_SLABS]
    w_refs = refs[N_SLABS:N_SLABS + 3]
    qg_ref, kg_ref, cos_ref, sin_ref, qt_ref, k_ref, vt_ref, rs_scr = refs[N_SLABS + 3:N_SLABS + 11]
    nl = tt // dilation
    ssq = x_refs[0][...] * x_refs[0][...]
    for c in range(1, N_SLABS):
        ssq = ssq + x_refs[c][...] * x_refs[c][...]
    rs = lax.rsqrt(jnp.sum(ssq, axis=-1, keepdims=True) * (1.0 / D_MODEL) + EPS)
    rs_scr[...] = jnp.broadcast_to(rs, (tt, LANES))

    if dilation == 16:
        a_scr, rsa_scr = refs[N_SLABS + 11:]
        quarter = tt // 4
        for q in range(4):
            dst = slice(q * quarter, (q + 1) * quarter)
            for c in range(N_SLABS):
                a_scr[c, dst, :] = x_refs[c][pl.ds(q, quarter, stride=4), :]
            rsa_scr[dst, :] = rs_scr[pl.ds(q, quarter, stride=4), :]

    def residue_rows(r, lo, n):
        if dilation == 1:
            idx = pl.ds(lo, n)
            return [x_refs[c][idx, :] for c in range(N_SLABS)], rs_scr[idx, :]
        if dilation == 4:
            idx = pl.ds(r + lo * 4, n, stride=4)
            return [x_refs[c][idx, :] for c in range(N_SLABS)], rs_scr[idx, :]
        idx = pl.ds((r % 4) * (tt // 4) + r // 4 + lo * 4, n, stride=4)
        return [a_scr[c, idx, :] for c in range(N_SLABS)], rsa_scr[idx, :]

    widen = lambda t: jnp.concatenate([t] * (QKV_CHUNK // LANES), axis=1)
    gains = (widen(qg_ref[...]), widen(kg_ref[...]))
    for ck in range(tt // QKV_CHUNK):
        if nl >= QKV_CHUNK:
            per = nl // QKV_CHUNK
            pieces = [(ck // per, (ck % per) * QKV_CHUNK, QKV_CHUNK)]
        else:
            pieces = [(ck * (QKV_CHUNK // nl) + i, 0, nl) for i in range(QKV_CHUNK // nl)]
        hs = []
        for r, lo, n in pieces:
            slabs, scale = residue_rows(r, lo, n)
            hs.append(jnp.concatenate([slabs[c] * scale for c in range(N_SLABS)], axis=1).astype(BF16))
        h = jnp.concatenate(hs, axis=0) if len(hs) > 1 else hs[0]
        ys = [lax.dot_general(w_ref[...], h, (((1,), (1,)), ((), ())), preferred_element_type=F32)
              for w_ref in w_refs]
        cols = slice(ck * QKV_CHUNK, (ck + 1) * QKV_CHUNK)
        cos = cos_ref[:, cols]
        sin = sin_ref[:, cols]
        rot = []
        for s in range(2):
            g1, g2 = gains[s][:HALF], gains[s][HALF:]
            c1, s2, c2, s1 = g1 * cos, g2 * sin, g2 * cos, g1 * sin
            parts = []
            for hh in range(HEADS):
                t = ys[s][hh * HEAD_DIM:(hh + 1) * HEAD_DIM, :]
                inv = lax.rsqrt(jnp.mean(t * t, axis=0, keepdims=True) + EPS)
                t1, t2 = t[:HALF], t[HALF:]
                parts += [(t1 * c1 - t2 * s2) * inv, (t2 * c2 + t1 * s1) * inv]
            rot.append(jnp.concatenate(parts, axis=0))
        q_t = rot[0].astype(BF16)
        k_rows = rot[1].T.astype(BF16)
        v_t = ys[2].astype(BF16)
        off = 0
        for r, lo, n in pieces:
            qt_ref[r, :, lo:lo + n] = q_t[:, off:off + n]
            k_ref[r, lo:lo + n, :] = k_rows[off:off + n, :]
            vt_ref[r, :, lo:lo + n] = v_t[:, off:off + n]
            off += n


def _qkv_project(x3, wt_all, group, qg, kg, cos_t, sin_t, dilation):
    tt = _qkv_step_tokens(dilation)
    n_sub = SEQ // dilation
    nl = tt // dilation
    slab = lambda c: pl.BlockSpec((None, tt, LANES), lambda b, j: (b, j, c))
    t_spec = pl.BlockSpec((None, dilation, ATTN_WIDTH, nl), lambda b, j: (b, 0, 0, j))
    r_spec = pl.BlockSpec((None, dilation, nl, ATTN_WIDTH), lambda b, j: (b, 0, j, 0))
    tab = pl.BlockSpec((None, HALF, tt), lambda b, j: (j, 0, 0))
    scratch = [pltpu.VMEM((tt, LANES), F32)]
    if dilation == 16:
        scratch += [pltpu.VMEM((N_SLABS, tt, LANES), F32), pltpu.VMEM((tt, LANES), F32)]
    return pl.pallas_call(
        functools.partial(_qkv_kernel, dilation=dilation, tt=tt),
        grid=(BATCH, SEQ // tt),
        in_specs=[slab(c) for c in range(N_SLABS)]
                 + [pl.BlockSpec((ATTN_WIDTH, D_MODEL), lambda b, j, s=s: (s * N_GROUPS + group, 0),
                                 pipeline_mode=pl.Buffered(1)) for s in range(3)]
                 + [_const_spec((HEAD_DIM, LANES)), _const_spec((HEAD_DIM, LANES)), tab, tab],
        out_specs=[t_spec, r_spec, t_spec],
        out_shape=[jax.ShapeDtypeStruct((BATCH, dilation, ATTN_WIDTH, n_sub), BF16),
                   jax.ShapeDtypeStruct((BATCH, dilation, n_sub, ATTN_WIDTH), BF16),
                   jax.ShapeDtypeStruct((BATCH, dilation, ATTN_WIDTH, n_sub), BF16)],
        scratch_shapes=scratch,
        compiler_params=_params(2),
        name=f"qkv_d{dilation}",
    )(*([x3] * N_SLABS), wt_all, wt_all, wt_all, qg, kg, cos_t, sin_t)


def _attn_kernel(qt_ref, kc_ref, kp_ref, vc_ref, vp_ref, o_ref, lse_ref, s_scr, p_scr, *, rb_count, qb):
    kj = lax.broadcasted_iota(jnp.int32, (2 * WIN, WIN), 0)
    qi = lax.broadcasted_iota(jnp.int32, (2 * WIN, WIN), 1)
    band = (kj >= qi) & (kj <= qi + WIN)
    bias_rest = jnp.where(band, 0.0, NEG).astype(F32)
    bias_first = jnp.where(band & (kj >= WIN), 0.0, NEG).astype(F32)
    bias_head = jnp.where(pl.program_id(2) > 0, bias_rest, bias_first)
    zeros = jnp.zeros((HEAD_DIM, WIN), BF16)
    ones = jnp.ones((BF16_ROWS, 2 * WIN), BF16)
    pad = jnp.zeros((LANES - HEADS, WIN), F32)
    units = [(rb, i, p) for rb in range(rb_count) for i in range(qb // WIN) for p in range(HEADS // 2)]

    def slices(i, p):
        return (slice(i * WIN, (i + 1) * WIN), slice((i - 1) * WIN, i * WIN),
                slice(p * 2 * HEAD_DIM, (p + 1) * 2 * HEAD_DIM))

    for n, (rb, i, p) in enumerate(units):
        cur, prev, cols = slices(i, p)
        k_prev = kp_ref[rb, :, cols] if i == 0 else kc_ref[rb, prev, cols]
        kk = jnp.concatenate([k_prev, kc_ref[rb, cur, cols]], axis=0)
        q2 = qt_ref[rb, cols, cur]
        qbd = jnp.concatenate(
            [jnp.concatenate([q2[:HEAD_DIM], zeros], axis=1),
             jnp.concatenate([zeros, q2[HEAD_DIM:]], axis=1)], axis=0)
        bias = bias_head if i == 0 else bias_rest
        s_scr[n] = (jnp.dot(kk, qbd, preferred_element_type=F32)
                    + jnp.concatenate([bias, bias], axis=1))

    stats = []
    for n in range(len(units)):
        s = s_scr[n]
        m = jnp.max(s, axis=0, keepdims=True)
        e = jnp.exp2(s - m)
        p_scr[n] = e.astype(BF16)
        stats.append(m)

    outs, lses = {}, {}
    for n, (rb, i, p) in enumerate(units):
        cur, prev, cols = slices(i, p)
        v_prev = vp_ref[rb, cols, :] if i == 0 else vc_ref[rb, cols, prev]
        vv = jnp.concatenate([v_prev, vc_ref[rb, cols, cur]], axis=1)
        o2 = jnp.dot(jnp.concatenate([vv, ones], axis=0), p_scr[n], preferred_element_type=F32)
        m, den = stats[n], o2[2 * HEAD_DIM:2 * HEAD_DIM + 1, :]
        inv = 1.0 / den
        outs.setdefault((rb, i), []).extend(
            [o2[:HEAD_DIM, :WIN] * inv[:, :WIN], o2[HEAD_DIM:2 * HEAD_DIM, WIN:] * inv[:, WIN:]])
        lse = m * LN2 + jnp.log(den)
        lses.setdefault((rb, i), []).extend([lse[:, :WIN], lse[:, WIN:]])
    for (rb, i), pieces in outs.items():
        cur = slice(i * WIN, (i + 1) * WIN)
        o_ref[cur, rb * ATTN_WIDTH:(rb + 1) * ATTN_WIDTH] = jnp.concatenate(pieces, axis=0).T
        lse_ref[cur, rb * LANES:(rb + 1) * LANES] = jnp.concatenate(lses[(rb, i)] + [pad], axis=0).T


def _attention(qt, k, vt, dilation):
    rb, qb = _group_tiles(dilation)
    n_sub = SEQ // dilation
    per = qb // WIN
    prev_blk = lambda j: jnp.maximum(j * per - 1, 0)
    out, lse = pl.pallas_call(
        functools.partial(_attn_kernel, rb_count=rb, qb=qb),
        grid=(BATCH, dilation // rb, n_sub // qb),
        in_specs=[pl.BlockSpec((None, rb, ATTN_WIDTH, qb), lambda b, r, j: (b, r, 0, j)),
                  pl.BlockSpec((None, rb, qb, ATTN_WIDTH), lambda b, r, j: (b, r, j, 0)),
                  pl.BlockSpec((None, rb, WIN, ATTN_WIDTH), lambda b, r, j: (b, r, prev_blk(j), 0)),
                  pl.BlockSpec((None, rb, ATTN_WIDTH, qb), lambda b, r, j: (b, r, 0, j)),
                  pl.BlockSpec((None, rb, ATTN_WIDTH, WIN), lambda b, r, j: (b, r, 0, prev_blk(j)))],
        out_specs=[pl.BlockSpec((None, qb, rb * ATTN_WIDTH), lambda b, r, j: (b, j, r)),
                   pl.BlockSpec((None, qb, rb * LANES), lambda b, r, j: (b, j, r))],
        out_shape=[jax.ShapeDtypeStruct((BATCH, n_sub, dilation * ATTN_WIDTH), F32),
                   jax.ShapeDtypeStruct((BATCH, n_sub, dilation * LANES), F32)],
        scratch_shapes=[pltpu.VMEM((rb * per * (HEADS // 2), 2 * WIN, 2 * WIN), F32),
                        pltpu.VMEM((rb * per * (HEADS // 2), 2 * WIN, 2 * WIN), BF16)],
        compiler_params=_params(3),
        name=f"attn_d{dilation}",
    )(qt, k, k, vt, vt)
    return out.reshape(BATCH * n_sub, dilation * ATTN_WIDTH), lse.reshape(BATCH * n_sub, dilation * LANES)


def _combine_kernel(x_ref, o0_ref, o1_ref, o2_ref, l0_ref, l1_ref, l2_ref, ex_ref, w_ref, mg_ref, w1_ref, w2_ref,
                    out_ref, o_scr, l_scr):
    n_slab = ATTN_WIDTH // LANES
    for gi, (o_ref, l_ref) in enumerate(((o1_ref, l1_ref), (o2_ref, l2_ref))):
        d = ATTN_GROUPS[gi + 1][1]
        n = ROW_TILE // d
        for r in range(d):
            rows = pl.ds(r, n, stride=d)
            for c in range(n_slab):
                o_scr[gi, c, rows, :] = o_ref[:, r * ATTN_WIDTH + c * LANES: r * ATTN_WIDTH + (c + 1) * LANES]
            l_scr[gi, rows, :] = l_ref[:, r * LANES:(r + 1) * LANES]
    os = [o0_ref[...]] + [jnp.concatenate([o_scr[gi, c] for c in range(n_slab)], axis=1) for gi in range(2)]
    ls = [l0_ref[...], l_scr[0], l_scr[1]]
    m = jnp.maximum(jnp.maximum(ls[0], ls[1]), ls[2])
    es = [jnp.exp(l - m) for l in ls]
    inv = 1.0 / (es[0] + es[1] + es[2])
    mixed = jnp.zeros((ROW_TILE, ATTN_WIDTH), F32)
    for e, o in zip(es, os):
        wide = jnp.dot((e * inv).astype(BF16), ex_ref[...], preferred_element_type=F32)
        mixed = mixed + wide * o
    x_mid = x_ref[...] + jnp.dot(mixed.astype(BF16), w_ref[...], preferred_element_type=F32)
    out_ref[...] = _mlp_rows(x_mid, mg_ref, w1_ref, w2_ref)


def _combine_project_mlp(xt, outs, lses, expand, w_out, mlp_g, w1, w2):
    t = xt.shape[0]
    assert ATTN_GROUPS[0][1] == 1
    dil = lambda w: [pl.BlockSpec((ROW_TILE // d, d * w), lambda i: (i, 0)) for _, d in ATTN_GROUPS]
    row = pl.BlockSpec((ROW_TILE, D_MODEL), lambda i: (i, 0))
    return pl.pallas_call(
        _combine_kernel,
        grid=(t // ROW_TILE,),
        in_specs=[row] + dil(ATTN_WIDTH) + dil(LANES)
                 + [_const_spec((LANES, ATTN_WIDTH)), _const_spec((ATTN_WIDTH, D_MODEL))] + _mlp_specs(),
        out_specs=row,
        out_shape=jax.ShapeDtypeStruct(xt.shape, F32),
        scratch_shapes=[pltpu.VMEM((N_GROUPS - 1, ATTN_WIDTH // LANES, ROW_TILE, LANES), F32),
                        pltpu.VMEM((N_GROUPS - 1, ROW_TILE, LANES), F32)],
        compiler_params=_params(1),
        name="attn_combine_mlp",
    )(xt, *outs, *lses, expand, w_out, mlp_g.reshape(1, D_MODEL), w1, w2)


def _rope_tables():
    pos = jnp.arange(SEQ, dtype=F32)
    inv_freq = ROPE_THETA ** (-jnp.arange(0, HEAD_DIM, 2, dtype=F32) / HEAD_DIM)
    ang = pos[:, None] * inv_freq[None, :]
    return jnp.cos(ang), jnp.sin(ang)


def _dilated_table(tab, dilation):
    tt = _qkv_step_tokens(dilation)
    nl = tt // dilation
    return tab.reshape(SEQ // tt, nl, dilation, HALF).transpose(0, 3, 2, 1).reshape(SEQ // tt, HALF, tt)


def _attention_layer(xt, g, w_in, q_gain, k_gain, w_out, cos, sin, mlp_g, w1, w2):
    x3 = xt.reshape(BATCH, SEQ, D_MODEL)
    wt_all = (w_in * g[:, None]).T.astype(BF16)
    lane_bcast = lambda v: jnp.broadcast_to(v.reshape(HEAD_DIM, 1), (HEAD_DIM, LANES))
    qg = lane_bcast(q_gain * (HEAD_DIM ** -0.5 * LOG2E))
    kg = lane_bcast(k_gain)
    head = jnp.arange(LANES)[:, None]
    lane = jnp.arange(ATTN_WIDTH)[None, :]
    expand = (lane // HEAD_DIM == head).astype(BF16)
    outs, lses = [], []
    for gi, (_, dilation) in enumerate(ATTN_GROUPS):
        qt, k, vt = _qkv_project(x3, wt_all, gi, qg, kg, _dilated_table(cos, dilation),
                                 _dilated_table(sin, dilation), dilation)
        o, l = _attention(qt, k, vt, dilation)
        outs.append(o)
        lses.append(l)
    return _combine_project_mlp(xt, outs, lses, expand, w_out.astype(BF16), mlp_g, w1, w2)


def kernel(x, mixer_norm, mlp_norm, conv_w_in, conv_b_in, conv_w_dw, conv_b_dw, conv_ln_g, conv_ln_b,
           conv_w_out, conv_b_out, attn_w_in, attn_q_norm, attn_k_norm, attn_w_out, mlp_w1, mlp_w2):
    assert x.shape == (BATCH, SEQ, D_MODEL) and x.dtype == F32
    cos, sin = _rope_tables()
    xt = x.reshape(BATCH * SEQ, D_MODEL)
    for i in range(DEPTH):
        j = i // 2
        mlp = (mlp_norm[i], mlp_w1[i].astype(BF16), mlp_w2[i].astype(BF16))
        if i % 2 == 0:
            x3 = _conv_layer(xt.reshape(BATCH, SEQ, D_MODEL), mixer_norm[i], conv_w_in[j].astype(BF16),
                             conv_b_in[j], conv_w_dw[j], conv_b_dw[j], conv_ln_g[j], conv_ln_b[j],
                             conv_w_out[j].astype(BF16), conv_b_out[j], *mlp)
            xt = x3.reshape(BATCH * SEQ, D_MODEL)
        else:
            xt = _attention_layer(xt, mixer_norm[i], attn_w_in[j], attn_q_norm[j], attn_k_norm[j],
                                  attn_w_out[j], cos, sin, *mlp)
    return xt.reshape(BATCH, SEQ, D_MODEL)
```

```python
import functools

import jax
import jax.numpy as jnp
from jax import lax
from jax.experimental import pallas as pl
from jax.experimental.pallas import tpu as pltpu

D_MODEL = 1024
BATCH = 8
SEQ = 4096
DEPTH = 4
CONV_KERNEL = 31
ATTN_GROUPS = ((128, 1), (512, 4), (2048, 16))
N_GROUPS = len(ATTN_GROUPS)
HEADS = 8
HEAD_DIM = 64
HALF = HEAD_DIM // 2
ATTN_WIDTH = HEADS * HEAD_DIM
D_FF = 4 * D_MODEL
ROPE_THETA = 10000.0
EPS = 1e-6

LANES = 128
BF16_ROWS = 16
WIN = 128
V7X_VMEM_LIMIT = 56 * 1024 * 1024

ROW_TILE = 512
CONV_HALO = 32
CONV_CHUNK = 64
CONV_PAIR = CONV_CHUNK // 2
CONV_GROUP = 4
N_SLABS = D_MODEL // LANES
CONV_SLABS = N_SLABS
QKV_CHUNK = 512
ATTN_STEP_TOKENS = 2048
FF_CHUNK = 1024

F32 = jnp.float32
BF16 = jnp.bfloat16
NEG = -1e30
LN2 = 0.6931471805599453
LOG2E = 1.4426950408889634


def _params(n_axes):
    return pltpu.CompilerParams(
        dimension_semantics=("arbitrary",) * n_axes,
        vmem_limit_bytes=V7X_VMEM_LIMIT,
    )


def _const_spec(shape):
    nd = len(shape)
    return pl.BlockSpec(shape, lambda *_: (0,) * nd, pipeline_mode=pl.Buffered(1))


def _layer_spec(shape, layer):
    nd = len(shape)
    return pl.BlockSpec((None,) + tuple(shape), lambda *_: (layer,) + (0,) * nd, pipeline_mode=pl.Buffered(1))


def _rms_rows(x, g):
    return x * lax.rsqrt(jnp.mean(x * x, axis=-1, keepdims=True) + EPS) * g


def _sigmoid(x):
    return 1.0 / (1.0 + jnp.exp(-x))


def _mlp_rows(x, g_ref, w1_ref, w2_ref):
    h = _rms_rows(x, g_ref[...]).astype(BF16)
    acc = x
    for c in range(D_FF // FF_CHUNK):
        cs = slice(c * FF_CHUNK, (c + 1) * FF_CHUNK)
        z = jnp.dot(h, w1_ref[:, cs], preferred_element_type=F32)
        z = jnp.maximum(z, 0.0)
        acc = acc + jnp.dot((z * z).astype(BF16), w2_ref[cs, :], preferred_element_type=F32)
    return acc


def _mlp_specs(layer):
    return [_const_spec((1, D_MODEL)), _layer_spec((D_MODEL, D_FF), layer), _layer_spec((D_FF, D_MODEL), layer)]


def _conv_kernel(x_ref, g_ref, win_ref, bin_ref, wdw_ref, bdw_ref, lng_ref, lnb_ref, wout_ref, bout_ref,
                 mg_ref, w1_ref, w2_ref, o_ref, ubuf, pkbuf, cbuf):
    x = x_ref[...]
    h = _rms_rows(x, g_ref[...]).astype(BF16)
    a = jnp.dot(h, win_ref[...], preferred_element_type=F32) + bin_ref[...]
    u = (a[:, :D_MODEL] * _sigmoid(a[:, D_MODEL:])).astype(BF16).astype(F32)

    @pl.when(pl.program_id(1) == 0)
    def _():
        ubuf[:, 0:CONV_HALO, :] = jnp.zeros((CONV_SLABS, CONV_HALO, LANES), F32)

    n_words = CONV_HALO + ROW_TILE - CONV_PAIR
    for c in range(CONV_SLABS):
        ubuf[c, CONV_HALO:CONV_HALO + ROW_TILE, :] = u[:, c * LANES:(c + 1) * LANES]
        bits = pltpu.bitcast(ubuf[c], jnp.uint32)
        pkbuf[c] = lax.shift_right_logical(bits[:n_words], jnp.uint32(16)) | bits[CONV_PAIR:CONV_PAIR + n_words]
    first_tap = CONV_HALO - (CONV_KERNEL - 1)
    high_half = jnp.uint32(0xFFFF0000)

    def slab(c, carry):
        for base in range(0, ROW_TILE, CONV_CHUNK):
            acc_lo = jnp.broadcast_to(bdw_ref[c], (CONV_PAIR, LANES))
            acc_hi = acc_lo
            for k0 in range(0, CONV_KERNEL, CONV_GROUP):
                part = None
                for k in range(k0, min(k0 + CONV_GROUP, CONV_KERNEL)):
                    lo = base + first_tap + k
                    win = pltpu.bitcast(pkbuf[c, lo:lo + CONV_PAIR, :], BF16)
                    wk = jnp.concatenate([wdw_ref[c, k]] * (CONV_CHUNK // BF16_ROWS), axis=0)
                    part = win * wk if part is None else part + win * wk
                words = pltpu.bitcast(part, jnp.uint32)
                acc_lo = acc_lo + pltpu.bitcast(lax.shift_left(words, jnp.uint32(16)), F32)
                acc_hi = acc_hi + pltpu.bitcast(words & high_half, F32)
            cbuf[c, base:base + CONV_PAIR, :] = acc_lo
            cbuf[c, base + CONV_PAIR:base + CONV_CHUNK, :] = acc_hi
        return carry

    lax.fori_loop(0, CONV_SLABS, slab, 0)
    for c in range(CONV_SLABS):
        ubuf[c, 0:CONV_HALO, :] = ubuf[c, ROW_TILE:ROW_TILE + CONV_HALO, :]
    conv = jnp.concatenate([cbuf[c] for c in range(CONV_SLABS)], axis=1)
    mu = jnp.mean(conv, axis=-1, keepdims=True)
    xc = conv - mu
    y = xc * lax.rsqrt(jnp.mean(xc * xc, axis=-1, keepdims=True) + EPS) * lng_ref[...] + lnb_ref[...]
    y = (y * _sigmoid(y)).astype(BF16)
    x_mid = x + jnp.dot(y, wout_ref[...], preferred_element_type=F32) + bout_ref[...]
    o_ref[...] = _mlp_rows(x_mid, mg_ref, w1_ref, w2_ref)


def _conv_layer(x3, g, w_in, b_in, w_dw, b_dw, ln_g, ln_b, w_out, b_out, mlp_g, w1, w2, j, layer):
    row = pl.BlockSpec((None, ROW_TILE, D_MODEL), lambda b, j: (b, j, 0))
    vec = lambda v: v.reshape(1, -1)
    w_slabs = jnp.broadcast_to(w_dw.reshape(CONV_KERNEL, CONV_SLABS, 1, LANES).transpose(1, 0, 2, 3),
                               (CONV_SLABS, CONV_KERNEL, BF16_ROWS, LANES)).astype(BF16)
    b_slabs = b_dw.reshape(CONV_SLABS, 1, LANES)
    return pl.pallas_call(
        _conv_kernel,
        grid=(BATCH, SEQ // ROW_TILE),
        in_specs=[row, _const_spec((1, D_MODEL)), _layer_spec((D_MODEL, 2 * D_MODEL), j),
                  _const_spec((1, 2 * D_MODEL)), _const_spec((CONV_SLABS, CONV_KERNEL, BF16_ROWS, LANES)),
                  _const_spec((CONV_SLABS, 1, LANES)), _const_spec((1, D_MODEL)), _const_spec((1, D_MODEL)),
                  _layer_spec((D_MODEL, D_MODEL), j), _const_spec((1, D_MODEL))] + _mlp_specs(layer),
        out_specs=row,
        out_shape=jax.ShapeDtypeStruct(x3.shape, F32),
        scratch_shapes=[pltpu.VMEM((CONV_SLABS, ROW_TILE + CONV_HALO, LANES), F32),
                        pltpu.VMEM((CONV_SLABS, ROW_TILE + CONV_HALO - CONV_PAIR, LANES), jnp.uint32),
                        pltpu.VMEM((CONV_SLABS, ROW_TILE, LANES), F32)],
        compiler_params=_params(2),
        name="conv_layer",
    )(x3, vec(g), w_in, vec(b_in), w_slabs, b_slabs, vec(ln_g), vec(ln_b), w_out, vec(b_out), vec(mlp_g), w1, w2)


def _group_tiles(dilation):
    n_sub = SEQ // dilation
    tl = min(n_sub, ATTN_STEP_TOKENS)
    return ATTN_STEP_TOKENS // tl, tl


def _qkv_step_tokens(dilation):
    return max(4 * QKV_CHUNK, WIN * dilation)


def _qkv_kernel(*refs, dilation, tt):
    x_refs = refs[:N_SLABS]
    w_refs = refs[N_SLABS:N_SLABS + 3]
    qg_ref, kg_ref, cos_ref, sin_ref, qt_ref, k_ref, vt_ref, rs_scr = refs[N_SLABS + 3:N_SLABS + 11]
    nl = tt // dilation
    ssq = x_refs[0][...] * x_refs[0][...]
    for c in range(1, N_SLABS):
        ssq = ssq + x_refs[c][...] * x_refs[c][...]
    rs = lax.rsqrt(jnp.sum(ssq, axis=-1, keepdims=True) * (1.0 / D_MODEL) + EPS)
    rs_scr[...] = jnp.broadcast_to(rs, (tt, LANES))

    if dilation == 16:
        a_scr, rsa_scr = refs[N_SLABS + 11:]
        quarter = tt // 4
        for q in range(4):
            dst = slice(q * quarter, (q + 1) * quarter)
            for c in range(N_SLABS):
                a_scr[c, dst, :] = x_refs[c][pl.ds(q, quarter, stride=4), :]
            rsa_scr[dst, :] = rs_scr[pl.ds(q, quarter, stride=4), :]

    def residue_rows(r, lo, n):
        if dilation == 1:
            idx = pl.ds(lo, n)
            return [x_refs[c][idx, :] for c in range(N_SLABS)], rs_scr[idx, :]
        if dilation == 4:
            idx = pl.ds(r + lo * 4, n, stride=4)
            return [x_refs[c][idx, :] for c in range(N_SLABS)], rs_scr[idx, :]
        idx = pl.ds((r % 4) * (tt // 4) + r // 4 + lo * 4, n, stride=4)
        return [a_scr[c, idx, :] for c in range(N_SLABS)], rsa_scr[idx, :]

    widen = lambda t: jnp.concatenate([t] * (QKV_CHUNK // LANES), axis=1)
    gains = (widen(qg_ref[...]), widen(kg_ref[...]))
    for ck in range(tt // QKV_CHUNK):
        if nl >= QKV_CHUNK:
            per = nl // QKV_CHUNK
            pieces = [(ck // per, (ck % per) * QKV_CHUNK, QKV_CHUNK)]
        else:
            pieces = [(ck * (QKV_CHUNK // nl) + i, 0, nl) for i in range(QKV_CHUNK // nl)]
        hs = []
        for r, lo, n in pieces:
            slabs, scale = residue_rows(r, lo, n)
            hs.append(jnp.concatenate([slabs[c] * scale for c in range(N_SLABS)], axis=1).astype(BF16))
        h = jnp.concatenate(hs, axis=0) if len(hs) > 1 else hs[0]
        ys = [lax.dot_general(w_ref[...], h, (((1,), (1,)), ((), ())), preferred_element_type=F32)
              for w_ref in w_refs]
        cols = slice(ck * QKV_CHUNK, (ck + 1) * QKV_CHUNK)
        cos = cos_ref[:, cols]
        sin = sin_ref[:, cols]
        rot = []
        for s in range(2):
            g1, g2 = gains[s][:HALF], gains[s][HALF:]
            c1, s2, c2, s1 = g1 * cos, g2 * sin, g2 * cos, g1 * sin
            parts = []
            for hh in range(HEADS):
                t = ys[s][hh * HEAD_DIM:(hh + 1) * HEAD_DIM, :]
                inv = lax.rsqrt(jnp.mean(t * t, axis=0, keepdims=True) + EPS)
                t1, t2 = t[:HALF], t[HALF:]
                parts += [(t1 * c1 - t2 * s2) * inv, (t2 * c2 + t1 * s1) * inv]
            rot.append(jnp.concatenate(parts, axis=0))
        q_t = rot[0].astype(BF16)
        k_rows = rot[1].T.astype(BF16)
        v_t = ys[2].astype(BF16)
        off = 0
        for r, lo, n in pieces:
            qt_ref[r, :, lo:lo + n] = q_t[:, off:off + n]
            k_ref[r, lo:lo + n, :] = k_rows[off:off + n, :]
            vt_ref[r, :, lo:lo + n] = v_t[:, off:off + n]
            off += n


def _qkv_project(x3, wt_all, j, group, qg, kg, cos_t, sin_t, dilation):
    tt = _qkv_step_tokens(dilation)
    n_sub = SEQ // dilation
    nl = tt // dilation
    slab = lambda c: pl.BlockSpec((None, tt, LANES), lambda b, j: (b, j, c))
    t_spec = pl.BlockSpec((None, dilation, ATTN_WIDTH, nl), lambda b, j: (b, 0, 0, j))
    r_spec = pl.BlockSpec((None, dilation, nl, ATTN_WIDTH), lambda b, j: (b, 0, j, 0))
    tab = pl.BlockSpec((None, HALF, tt), lambda b, j: (j, 0, 0))
    scratch = [pltpu.VMEM((tt, LANES), F32)]
    if dilation == 16:
        scratch += [pltpu.VMEM((N_SLABS, tt, LANES), F32), pltpu.VMEM((tt, LANES), F32)]
    return pl.pallas_call(
        functools.partial(_qkv_kernel, dilation=dilation, tt=tt),
        grid=(BATCH, SEQ // tt),
        in_specs=[slab(c) for c in range(N_SLABS)]
                 + [pl.BlockSpec((None, ATTN_WIDTH, D_MODEL), lambda b, i, s=s: (j, s * N_GROUPS + group, 0),
                                 pipeline_mode=pl.Buffered(1)) for s in range(3)]
                 + [_const_spec((HEAD_DIM, LANES)), _const_spec((HEAD_DIM, LANES)), tab, tab],
        out_specs=[t_spec, r_spec, t_spec],
        out_shape=[jax.ShapeDtypeStruct((BATCH, dilation, ATTN_WIDTH, n_sub), BF16),
                   jax.ShapeDtypeStruct((BATCH, dilation, n_sub, ATTN_WIDTH), BF16),
                   jax.ShapeDtypeStruct((BATCH, dilation, ATTN_WIDTH, n_sub), BF16)],
        scratch_shapes=scratch,
        compiler_params=_params(2),
        name=f"qkv_d{dilation}",
    )(*([x3] * N_SLABS), wt_all, wt_all, wt_all, qg, kg, cos_t, sin_t)


def _attn_kernel(qt_ref, kc_ref, kp_ref, vc_ref, vp_ref, o_ref, lse_ref, s_scr, p_scr, *, rb_count, qb):
    kj = lax.broadcasted_iota(jnp.int32, (2 * WIN, WIN), 0)
    qi = lax.broadcasted_iota(jnp.int32, (2 * WIN, WIN), 1)
    band = (kj >= qi) & (kj <= qi + WIN)
    bias_rest = jnp.where(band, 0.0, NEG).astype(F32)
    bias_first = jnp.where(band & (kj >= WIN), 0.0, NEG).astype(F32)
    bias_head = jnp.where(pl.program_id(2) > 0, bias_rest, bias_first)
    zeros = jnp.zeros((HEAD_DIM, WIN), BF16)
    ones = jnp.ones((BF16_ROWS, 2 * WIN), BF16)
    pad = jnp.zeros((LANES - HEADS, WIN), F32)
    units = [(rb, i, p) for rb in range(rb_count) for i in range(qb // WIN) for p in range(HEADS // 2)]

    def slices(i, p):
        return (slice(i * WIN, (i + 1) * WIN), slice((i - 1) * WIN, i * WIN),
                slice(p * 2 * HEAD_DIM, (p + 1) * 2 * HEAD_DIM))

    for n, (rb, i, p) in enumerate(units):
        cur, prev, cols = slices(i, p)
        k_prev = kp_ref[rb, :, cols] if i == 0 else kc_ref[rb, prev, cols]
        kk = jnp.concatenate([k_prev, kc_ref[rb, cur, cols]], axis=0)
        q2 = qt_ref[rb, cols, cur]
        qbd = jnp.concatenate(
            [jnp.concatenate([q2[:HEAD_DIM], zeros], axis=1),
             jnp.concatenate([zeros, q2[HEAD_DIM:]], axis=1)], axis=0)
        bias = bias_head if i == 0 else bias_rest
        s_scr[n] = (jnp.dot(kk, qbd, preferred_element_type=F32)
                    + jnp.concatenate([bias, bias], axis=1))

    stats = []
    for n in range(len(units)):
        s = s_scr[n]
        m = jnp.max(s, axis=0, keepdims=True)
        e = jnp.exp2(s - m)
        p_scr[n] = e.astype(BF16)
        stats.append(m)

    outs, lses = {}, {}
    for n, (rb, i, p) in enumerate(units):
        cur, prev, cols = slices(i, p)
        v_prev = vp_ref[rb, cols, :] if i == 0 else vc_ref[rb, cols, prev]
        vv = jnp.concatenate([v_prev, vc_ref[rb, cols, cur]], axis=1)
        o2 = jnp.dot(jnp.concatenate([vv, ones], axis=0), p_scr[n], preferred_element_type=F32)
        m, den = stats[n], o2[2 * HEAD_DIM:2 * HEAD_DIM + 1, :]
        inv = 1.0 / den
        outs.setdefault((rb, i), []).extend(
            [o2[:HEAD_DIM, :WIN] * inv[:, :WIN], o2[HEAD_DIM:2 * HEAD_DIM, WIN:] * inv[:, WIN:]])
        lse = m * LN2 + jnp.log(den)
        lses.setdefault((rb, i), []).extend([lse[:, :WIN], lse[:, WIN:]])
    for (rb, i), pieces in outs.items():
        cur = slice(i * WIN, (i + 1) * WIN)
        o_ref[cur, rb * ATTN_WIDTH:(rb + 1) * ATTN_WIDTH] = jnp.concatenate(pieces, axis=0).T
        lse_ref[cur, rb * LANES:(rb + 1) * LANES] = jnp.concatenate(lses[(rb, i)] + [pad], axis=0).T


def _attention(qt, k, vt, dilation):
    rb, qb = _group_tiles(dilation)
    n_sub = SEQ // dilation
    per = qb // WIN
    prev_blk = lambda j: jnp.maximum(j * per - 1, 0)
    out, lse = pl.pallas_call(
        functools.partial(_attn_kernel, rb_count=rb, qb=qb),
        grid=(BATCH, dilation // rb, n_sub // qb),
        in_specs=[pl.BlockSpec((None, rb, ATTN_WIDTH, qb), lambda b, r, j: (b, r, 0, j)),
                  pl.BlockSpec((None, rb, qb, ATTN_WIDTH), lambda b, r, j: (b, r, j, 0)),
                  pl.BlockSpec((None, rb, WIN, ATTN_WIDTH), lambda b, r, j: (b, r, prev_blk(j), 0)),
                  pl.BlockSpec((None, rb, ATTN_WIDTH, qb), lambda b, r, j: (b, r, 0, j)),
                  pl.BlockSpec((None, rb, ATTN_WIDTH, WIN), lambda b, r, j: (b, r, 0, prev_blk(j)))],
        out_specs=[pl.BlockSpec((None, qb, rb * ATTN_WIDTH), lambda b, r, j: (b, j, r)),
                   pl.BlockSpec((None, qb, rb * LANES), lambda b, r, j: (b, j, r))],
        out_shape=[jax.ShapeDtypeStruct((BATCH, n_sub, dilation * ATTN_WIDTH), F32),
                   jax.ShapeDtypeStruct((BATCH, n_sub, dilation * LANES), F32)],
        scratch_shapes=[pltpu.VMEM((rb * per * (HEADS // 2), 2 * WIN, 2 * WIN), F32),
                        pltpu.VMEM((rb * per * (HEADS // 2), 2 * WIN, 2 * WIN), BF16)],
        compiler_params=_params(3),
        name=f"attn_d{dilation}",
    )(qt, k, k, vt, vt)
    return out.reshape(BATCH * n_sub, dilation * ATTN_WIDTH), lse.reshape(BATCH * n_sub, dilation * LANES)


def _combine_kernel(x_ref, o0_ref, o1_ref, o2_ref, l0_ref, l1_ref, l2_ref, ex_ref, w_ref, mg_ref, w1_ref, w2_ref,
                    out_ref, o_scr, l_scr):
    n_slab = ATTN_WIDTH // LANES
    for gi, (o_ref, l_ref) in enumerate(((o1_ref, l1_ref), (o2_ref, l2_ref))):
        d = ATTN_GROUPS[gi + 1][1]
        n = ROW_TILE // d
        for r in range(d):
            rows = pl.ds(r, n, stride=d)
            for c in range(n_slab):
                o_scr[gi, c, rows, :] = o_ref[:, r * ATTN_WIDTH + c * LANES: r * ATTN_WIDTH + (c + 1) * LANES]
            l_scr[gi, rows, :] = l_ref[:, r * LANES:(r + 1) * LANES]
    os = [o0_ref[...]] + [jnp.concatenate([o_scr[gi, c] for c in range(n_slab)], axis=1) for gi in range(2)]
    ls = [l0_ref[...], l_scr[0], l_scr[1]]
    m = jnp.maximum(jnp.maximum(ls[0], ls[1]), ls[2])
    es = [jnp.exp(l - m) for l in ls]
    inv = 1.0 / (es[0] + es[1] + es[2])
    mixed = jnp.zeros((ROW_TILE, ATTN_WIDTH), F32)
    for e, o in zip(es, os):
        wide = jnp.dot((e * inv).astype(BF16), ex_ref[...], preferred_element_type=F32)
        mixed = mixed + wide * o
    x_mid = x_ref[...] + jnp.dot(mixed.astype(BF16), w_ref[...], preferred_element_type=F32)
    out_ref[...] = _mlp_rows(x_mid, mg_ref, w1_ref, w2_ref)


def _combine_project_mlp(xt, outs, lses, expand, w_out, mlp_g, w1, w2, j, layer):
    t = xt.shape[0]
    assert ATTN_GROUPS[0][1] == 1
    dil = lambda w: [pl.BlockSpec((ROW_TILE // d, d * w), lambda i: (i, 0)) for _, d in ATTN_GROUPS]
    row = pl.BlockSpec((ROW_TILE, D_MODEL), lambda i: (i, 0))
    return pl.pallas_call(
        _combine_kernel,
        grid=(t // ROW_TILE,),
        in_specs=[row] + dil(ATTN_WIDTH) + dil(LANES)
                 + [_const_spec((LANES, ATTN_WIDTH)), _layer_spec((ATTN_WIDTH, D_MODEL), j)] + _mlp_specs(layer),
        out_specs=row,
        out_shape=jax.ShapeDtypeStruct(xt.shape, F32),
        scratch_shapes=[pltpu.VMEM((N_GROUPS - 1, ATTN_WIDTH // LANES, ROW_TILE, LANES), F32),
                        pltpu.VMEM((N_GROUPS - 1, ROW_TILE, LANES), F32)],
        compiler_params=_params(1),
        name="attn_combine_mlp",
    )(xt, *outs, *lses, expand, w_out, mlp_g.reshape(1, D_MODEL), w1, w2)


def _rope_tables():
    pos = jnp.arange(SEQ, dtype=F32)
    inv_freq = ROPE_THETA ** (-jnp.arange(0, HEAD_DIM, 2, dtype=F32) / HEAD_DIM)
    ang = pos[:, None] * inv_freq[None, :]
    return jnp.cos(ang), jnp.sin(ang)


def _dilated_table(tab, dilation):
    tt = _qkv_step_tokens(dilation)
    nl = tt // dilation
    return tab.reshape(SEQ // tt, nl, dilation, HALF).transpose(0, 3, 2, 1).reshape(SEQ // tt, HALF, tt)


def _attention_layer(xt, wt_all, q_gain, k_gain, w_out, cos, sin, mlp_g, w1, w2, j, layer):
    x3 = xt.reshape(BATCH, SEQ, D_MODEL)
    lane_bcast = lambda v: jnp.broadcast_to(v.reshape(HEAD_DIM, 1), (HEAD_DIM, LANES))
    qg = lane_bcast(q_gain * (HEAD_DIM ** -0.5 * LOG2E))
    kg = lane_bcast(k_gain)
    head = jnp.arange(LANES)[:, None]
    lane = jnp.arange(ATTN_WIDTH)[None, :]
    expand = (lane // HEAD_DIM == head).astype(BF16)
    outs, lses = [], []
    for gi, (_, dilation) in enumerate(ATTN_GROUPS):
        qt, k, vt = _qkv_project(x3, wt_all, j, gi, qg, kg, _dilated_table(cos, dilation),
                                 _dilated_table(sin, dilation), dilation)
        o, l = _attention(qt, k, vt, dilation)
        outs.append(o)
        lses.append(l)
    return _combine_project_mlp(xt, outs, lses, expand, w_out, mlp_g, w1, w2, j, layer)


def kernel(x, mixer_norm, mlp_norm, conv_w_in, conv_b_in, conv_w_dw, conv_b_dw, conv_ln_g, conv_ln_b,
           conv_w_out, conv_b_out, attn_w_in, attn_q_norm, attn_k_norm, attn_w_out, mlp_w1, mlp_w2):
    assert x.shape == (BATCH, SEQ, D_MODEL) and x.dtype == F32
    cos, sin = _rope_tables()
    w1_all, w2_all = mlp_w1.astype(BF16), mlp_w2.astype(BF16)
    conv_in_all, conv_out_all = conv_w_in.astype(BF16), conv_w_out.astype(BF16)
    attn_out_all = attn_w_out.astype(BF16)
    attn_gain = mixer_norm[1::2]
    attn_wt_all = (attn_w_in * attn_gain[:, :, None]).transpose(0, 2, 1).astype(BF16)
    xt = x.reshape(BATCH * SEQ, D_MODEL)
    for i in range(DEPTH):
        j = i // 2
        if i % 2 == 0:
            x3 = _conv_layer(xt.reshape(BATCH, SEQ, D_MODEL), mixer_norm[i], conv_in_all, conv_b_in[j],
                             conv_w_dw[j], conv_b_dw[j], conv_ln_g[j], conv_ln_b[j], conv_out_all, conv_b_out[j],
                             mlp_norm[i], w1_all, w2_all, j, i)
            xt = x3.reshape(BATCH * SEQ, D_MODEL)
        else:
            xt = _attention_layer(xt, attn_wt_all, attn_q_norm[j], attn_k_norm[j], attn_out_all, cos, sin,
                                  mlp_norm[i], w1_all, w2_all, j, i)
    return xt.reshape(BATCH, SEQ, D_MODEL)
```

```python
import functools

import jax
import jax.numpy as jnp
from jax import lax
from jax.experimental import pallas as pl
from jax.experimental.pallas import tpu as pltpu

D_MODEL = 1024
BATCH = 8
SEQ = 4096
DEPTH = 4
CONV_KERNEL = 31
ATTN_GROUPS = ((128, 1), (512, 4), (2048, 16))
N_GROUPS = len(ATTN_GROUPS)
HEADS = 8
HEAD_DIM = 64
HALF = HEAD_DIM // 2
ATTN_WIDTH = HEADS * HEAD_DIM
D_FF = 4 * D_MODEL
ROPE_THETA = 10000.0
EPS = 1e-6

LANES = 128
BF16_ROWS = 16
WIN = 128
V7X_VMEM_LIMIT = 56 * 1024 * 1024

ROW_TILE = 512
CONV_HALO = 32
CONV_CHUNK = 64
CONV_PAIR = CONV_CHUNK // 2
CONV_GROUP = 4
N_SLABS = D_MODEL // LANES
CONV_SLABS = N_SLABS
QKV_CHUNK = 512
ATTN_STEP_TOKENS = 2048
FF_CHUNK = 1024

F32 = jnp.float32
BF16 = jnp.bfloat16
NEG = -1e30
LN2 = 0.6931471805599453
LOG2E = 1.4426950408889634


def _params(n_axes):
    return pltpu.CompilerParams(
        dimension_semantics=("arbitrary",) * n_axes,
        vmem_limit_bytes=V7X_VMEM_LIMIT,
    )


def _const_spec(shape):
    nd = len(shape)
    return pl.BlockSpec(shape, lambda *_: (0,) * nd, pipeline_mode=pl.Buffered(1))


def _layer_spec(shape, layer):
    nd = len(shape)
    return pl.BlockSpec((None,) + tuple(shape), lambda *_: (layer,) + (0,) * nd, pipeline_mode=pl.Buffered(1))


def _rms_rows(x, g):
    return x * lax.rsqrt(jnp.mean(x * x, axis=-1, keepdims=True) + EPS) * g


def _sigmoid(x):
    return 1.0 / (1.0 + jnp.exp(-x))


def _mlp_rows(x, g_ref, w1_ref, w2_ref):
    h = _rms_rows(x, g_ref[...]).astype(BF16)
    acc = x
    for c in range(D_FF // FF_CHUNK):
        cs = slice(c * FF_CHUNK, (c + 1) * FF_CHUNK)
        z = jnp.dot(h, w1_ref[:, cs], preferred_element_type=F32)
        z = jnp.maximum(z, 0.0)
        acc = acc + jnp.dot((z * z).astype(BF16), w2_ref[cs, :], preferred_element_type=F32)
    return acc


def _mlp_specs(layer):
    return [_const_spec((1, D_MODEL)), _layer_spec((D_MODEL, D_FF), layer), _layer_spec((D_FF, D_MODEL), layer)]


def _conv_kernel(x_ref, g_ref, win_ref, bin_ref, wdw_ref, bdw_ref, lng_ref, lnb_ref, wout_ref, bout_ref,
                 mg_ref, w1_ref, w2_ref, o_ref, ubuf, pkbuf, cbuf):
    x = x_ref[...]
    h = _rms_rows(x, g_ref[...]).astype(BF16)
    a = jnp.dot(h, win_ref[...], preferred_element_type=F32) + bin_ref[...]
    u = (a[:, :D_MODEL] * _sigmoid(a[:, D_MODEL:])).astype(BF16).astype(F32)

    @pl.when(pl.program_id(1) == 0)
    def _():
        ubuf[:, 0:CONV_HALO, :] = jnp.zeros((CONV_SLABS, CONV_HALO, LANES), F32)

    n_words = CONV_HALO + ROW_TILE - CONV_PAIR
    for c in range(CONV_SLABS):
        ubuf[c, CONV_HALO:CONV_HALO + ROW_TILE, :] = u[:, c * LANES:(c + 1) * LANES]
        bits = pltpu.bitcast(ubuf[c], jnp.uint32)
        pkbuf[c] = lax.shift_right_logical(bits[:n_words], jnp.uint32(16)) | bits[CONV_PAIR:CONV_PAIR + n_words]
    first_tap = CONV_HALO - (CONV_KERNEL - 1)
    high_half = jnp.uint32(0xFFFF0000)

    def slab(c, carry):
        for base in range(0, ROW_TILE, CONV_CHUNK):
            acc_lo = jnp.broadcast_to(bdw_ref[c], (CONV_PAIR, LANES))
            acc_hi = acc_lo
            for k0 in range(0, CONV_KERNEL, CONV_GROUP):
                part = None
                for k in range(k0, min(k0 + CONV_GROUP, CONV_KERNEL)):
                    lo = base + first_tap + k
                    win = pltpu.bitcast(pkbuf[c, lo:lo + CONV_PAIR, :], BF16)
                    wk = jnp.concatenate([wdw_ref[c, k]] * (CONV_CHUNK // BF16_ROWS), axis=0)
                    part = win * wk if part is None else part + win * wk
                words = pltpu.bitcast(part, jnp.uint32)
                acc_lo = acc_lo + pltpu.bitcast(lax.shift_left(words, jnp.uint32(16)), F32)
                acc_hi = acc_hi + pltpu.bitcast(words & high_half, F32)
            cbuf[c, base:base + CONV_PAIR, :] = acc_lo
            cbuf[c, base + CONV_PAIR:base + CONV_CHUNK, :] = acc_hi
        return carry

    lax.fori_loop(0, CONV_SLABS, slab, 0)
    for c in range(CONV_SLABS):
        ubuf[c, 0:CONV_HALO, :] = ubuf[c, ROW_TILE:ROW_TILE + CONV_HALO, :]
    conv = jnp.concatenate([cbuf[c] for c in range(CONV_SLABS)], axis=1)
    mu = jnp.mean(conv, axis=-1, keepdims=True)
    xc = conv - mu
    y = xc * lax.rsqrt(jnp.mean(xc * xc, axis=-1, keepdims=True) + EPS) * lng_ref[...] + lnb_ref[...]
    y = (y * _sigmoid(y)).astype(BF16)
    x_mid = x + jnp.dot(y, wout_ref[...], preferred_element_type=F32) + bout_ref[...]
    o_ref[...] = _mlp_rows(x_mid, mg_ref, w1_ref, w2_ref)


def _conv_layer(x3, g, w_in, b_in, w_dw, b_dw, ln_g, ln_b, w_out, b_out, mlp_g, w1, w2, j, layer):
    row = pl.BlockSpec((None, ROW_TILE, D_MODEL), lambda b, j: (b, j, 0))
    vec = lambda v: v.reshape(1, -1)
    w_slabs = jnp.broadcast_to(w_dw.reshape(CONV_KERNEL, CONV_SLABS, 1, LANES).transpose(1, 0, 2, 3),
                               (CONV_SLABS, CONV_KERNEL, BF16_ROWS, LANES)).astype(BF16)
    b_slabs = b_dw.reshape(CONV_SLABS, 1, LANES)
    return pl.pallas_call(
        _conv_kernel,
        grid=(BATCH, SEQ // ROW_TILE),
        in_specs=[row, _const_spec((1, D_MODEL)), _layer_spec((D_MODEL, 2 * D_MODEL), j),
                  _const_spec((1, 2 * D_MODEL)), _const_spec((CONV_SLABS, CONV_KERNEL, BF16_ROWS, LANES)),
                  _const_spec((CONV_SLABS, 1, LANES)), _const_spec((1, D_MODEL)), _const_spec((1, D_MODEL)),
                  _layer_spec((D_MODEL, D_MODEL), j), _const_spec((1, D_MODEL))] + _mlp_specs(layer),
        out_specs=row,
        out_shape=jax.ShapeDtypeStruct(x3.shape, F32),
        scratch_shapes=[pltpu.VMEM((CONV_SLABS, ROW_TILE + CONV_HALO, LANES), F32),
                        pltpu.VMEM((CONV_SLABS, ROW_TILE + CONV_HALO - CONV_PAIR, LANES), jnp.uint32),
                        pltpu.VMEM((CONV_SLABS, ROW_TILE, LANES), F32)],
        compiler_params=_params(2),
        name="conv_layer",
    )(x3, vec(g), w_in, vec(b_in), w_slabs, b_slabs, vec(ln_g), vec(ln_b), w_out, vec(b_out), vec(mlp_g), w1, w2)


def _group_tiles(dilation):
    n_sub = SEQ // dilation
    tl = min(n_sub, ATTN_STEP_TOKENS)
    return ATTN_STEP_TOKENS // tl, tl


def _qkv_step_tokens(dilation):
    return max(4 * QKV_CHUNK, WIN * dilation)


def _qkv_kernel(*refs, dilation, tt):
    x_refs = refs[:N_SLABS]
    w_refs = refs[N_SLABS:N_SLABS + 3]
    qg_ref, kg_ref, cos_ref, sin_ref, qt_ref, k_ref, vt_ref, rs_scr = refs[N_SLABS + 3:N_SLABS + 11]
    nl = tt // dilation
    ssq = x_refs[0][...] * x_refs[0][...]
    for c in range(1, N_SLABS):
        ssq = ssq + x_refs[c][...] * x_refs[c][...]
    rs = lax.rsqrt(jnp.sum(ssq, axis=-1, keepdims=True) * (1.0 / D_MODEL) + EPS)
    rs_scr[...] = jnp.broadcast_to(rs, (tt, LANES))

    if dilation == 16:
        a_scr, rsa_scr = refs[N_SLABS + 11:]
        quarter = tt // 4
        for q in range(4):
            dst = slice(q * quarter, (q + 1) * quarter)
            for c in range(N_SLABS):
                a_scr[c, dst, :] = x_refs[c][pl.ds(q, quarter, stride=4), :]
            rsa_scr[dst, :] = rs_scr[pl.ds(q, quarter, stride=4), :]

    def residue_rows(r, lo, n):
        if dilation == 1:
            idx = pl.ds(lo, n)
            return [x_refs[c][idx, :] for c in range(N_SLABS)], rs_scr[idx, :]
        if dilation == 4:
            idx = pl.ds(r + lo * 4, n, stride=4)
            return [x_refs[c][idx, :] for c in range(N_SLABS)], rs_scr[idx, :]
        idx = pl.ds((r % 4) * (tt // 4) + r // 4 + lo * 4, n, stride=4)
        return [a_scr[c, idx, :] for c in range(N_SLABS)], rsa_scr[idx, :]

    widen = lambda t: jnp.concatenate([t] * (QKV_CHUNK // LANES), axis=1)
    gains = (widen(qg_ref[...]), widen(kg_ref[...]))
    for ck in range(tt // QKV_CHUNK):
        if nl >= QKV_CHUNK:
            per = nl // QKV_CHUNK
            pieces = [(ck // per, (ck % per) * QKV_CHUNK, QKV_CHUNK)]
        else:
            pieces = [(ck * (QKV_CHUNK // nl) + i, 0, nl) for i in range(QKV_CHUNK // nl)]
        hs = []
        for r, lo, n in pieces:
            slabs, scale = residue_rows(r, lo, n)
            hs.append(jnp.concatenate([slabs[c] * scale for c in range(N_SLABS)], axis=1).astype(BF16))
        h = jnp.concatenate(hs, axis=0) if len(hs) > 1 else hs[0]
        ys = [lax.dot_general(w_ref[...], h, (((0,), (1,)), ((), ())), preferred_element_type=F32)
              for w_ref in w_refs]
        cols = slice(ck * QKV_CHUNK, (ck + 1) * QKV_CHUNK)
        cos = cos_ref[:, cols]
        sin = sin_ref[:, cols]
        rot = []
        for s in range(2):
            g1, g2 = gains[s][:HALF], gains[s][HALF:]
            c1, s2, c2, s1 = g1 * cos, g2 * sin, g2 * cos, g1 * sin
            parts = []
            for hh in range(HEADS):
                t = ys[s][hh * HEAD_DIM:(hh + 1) * HEAD_DIM, :]
                inv = lax.rsqrt(jnp.mean(t * t, axis=0, keepdims=True) + EPS)
                t1, t2 = t[:HALF], t[HALF:]
                parts += [(t1 * c1 - t2 * s2) * inv, (t2 * c2 + t1 * s1) * inv]
            rot.append(jnp.concatenate(parts, axis=0))
        q_t = rot[0].astype(BF16)
        k_rows = rot[1].T.astype(BF16)
        v_t = ys[2].astype(BF16)
        off = 0
        for r, lo, n in pieces:
            qt_ref[r, :, lo:lo + n] = q_t[:, off:off + n]
            k_ref[r, lo:lo + n, :] = k_rows[off:off + n, :]
            vt_ref[r, :, lo:lo + n] = v_t[:, off:off + n]
            off += n


def _qkv_project(x3, w_all, j, group, qg, kg, cos_t, sin_t, dilation):
    tt = _qkv_step_tokens(dilation)
    n_sub = SEQ // dilation
    nl = tt // dilation
    slab = lambda c: pl.BlockSpec((None, tt, LANES), lambda b, j: (b, j, c))
    t_spec = pl.BlockSpec((None, dilation, ATTN_WIDTH, nl), lambda b, j: (b, 0, 0, j))
    r_spec = pl.BlockSpec((None, dilation, nl, ATTN_WIDTH), lambda b, j: (b, 0, j, 0))
    tab = pl.BlockSpec((None, HALF, tt), lambda b, j: (j, 0, 0))
    scratch = [pltpu.VMEM((tt, LANES), F32)]
    if dilation == 16:
        scratch += [pltpu.VMEM((N_SLABS, tt, LANES), F32), pltpu.VMEM((tt, LANES), F32)]
    return pl.pallas_call(
        functools.partial(_qkv_kernel, dilation=dilation, tt=tt),
        grid=(BATCH, SEQ // tt),
        in_specs=[slab(c) for c in range(N_SLABS)]
                 + [pl.BlockSpec((None, D_MODEL, ATTN_WIDTH), lambda b, i, s=s: (j, 0, s * N_GROUPS + group),
                                 pipeline_mode=pl.Buffered(1)) for s in range(3)]
                 + [_const_spec((HEAD_DIM, LANES)), _const_spec((HEAD_DIM, LANES)), tab, tab],
        out_specs=[t_spec, r_spec, t_spec],
        out_shape=[jax.ShapeDtypeStruct((BATCH, dilation, ATTN_WIDTH, n_sub), BF16),
                   jax.ShapeDtypeStruct((BATCH, dilation, n_sub, ATTN_WIDTH), BF16),
                   jax.ShapeDtypeStruct((BATCH, dilation, ATTN_WIDTH, n_sub), BF16)],
        scratch_shapes=scratch,
        compiler_params=_params(2),
        name=f"qkv_d{dilation}",
    )(*([x3] * N_SLABS), w_all, w_all, w_all, qg, kg, cos_t, sin_t)


def _attn_kernel(qt_ref, kc_ref, kp_ref, vc_ref, vp_ref, o_ref, lse_ref, s_scr, p_scr, *, rb_count, qb):
    kj = lax.broadcasted_iota(jnp.int32, (2 * WIN, WIN), 0)
    qi = lax.broadcasted_iota(jnp.int32, (2 * WIN, WIN), 1)
    band = (kj >= qi) & (kj <= qi + WIN)
    bias_rest = jnp.where(band, 0.0, NEG).astype(F32)
    bias_first = jnp.where(band & (kj >= WIN), 0.0, NEG).astype(F32)
    bias_head = jnp.where(pl.program_id(2) > 0, bias_rest, bias_first)
    zeros = jnp.zeros((HEAD_DIM, WIN), BF16)
    ones = jnp.ones((BF16_ROWS, 2 * WIN), BF16)
    pad = jnp.zeros((LANES - HEADS, WIN), F32)
    units = [(rb, i, p) for rb in range(rb_count) for i in range(qb // WIN) for p in range(HEADS // 2)]

    def slices(i, p):
        return (slice(i * WIN, (i + 1) * WIN), slice((i - 1) * WIN, i * WIN),
                slice(p * 2 * HEAD_DIM, (p + 1) * 2 * HEAD_DIM))

    for n, (rb, i, p) in enumerate(units):
        cur, prev, cols = slices(i, p)
        k_prev = kp_ref[rb, :, cols] if i == 0 else kc_ref[rb, prev, cols]
        kk = jnp.concatenate([k_prev, kc_ref[rb, cur, cols]], axis=0)
        q2 = qt_ref[rb, cols, cur]
        qbd = jnp.concatenate(
            [jnp.concatenate([q2[:HEAD_DIM], zeros], axis=1),
             jnp.concatenate([zeros, q2[HEAD_DIM:]], axis=1)], axis=0)
        bias = bias_head if i == 0 else bias_rest
        s_scr[n] = (jnp.dot(kk, qbd, preferred_element_type=F32)
                    + jnp.concatenate([bias, bias], axis=1))

    stats = []
    for n in range(len(units)):
        s = s_scr[n]
        m = jnp.max(s, axis=0, keepdims=True)
        e = jnp.exp2(s - m)
        p_scr[n] = e.astype(BF16)
        stats.append(m)

    outs, lses = {}, {}
    for n, (rb, i, p) in enumerate(units):
        cur, prev, cols = slices(i, p)
        v_prev = vp_ref[rb, cols, :] if i == 0 else vc_ref[rb, cols, prev]
        vv = jnp.concatenate([v_prev, vc_ref[rb, cols, cur]], axis=1)
        o2 = jnp.dot(jnp.concatenate([vv, ones], axis=0), p_scr[n], preferred_element_type=F32)
        m, den = stats[n], o2[2 * HEAD_DIM:2 * HEAD_DIM + 1, :]
        inv = 1.0 / den
        outs.setdefault((rb, i), []).extend(
            [o2[:HEAD_DIM, :WIN] * inv[:, :WIN], o2[HEAD_DIM:2 * HEAD_DIM, WIN:] * inv[:, WIN:]])
        lse = m * LN2 + jnp.log(den)
        lses.setdefault((rb, i), []).extend([lse[:, :WIN], lse[:, WIN:]])
    for (rb, i), pieces in outs.items():
        cur = slice(i * WIN, (i + 1) * WIN)
        o_ref[cur, rb * ATTN_WIDTH:(rb + 1) * ATTN_WIDTH] = jnp.concatenate(pieces, axis=0).T
        lse_ref[cur, rb * LANES:(rb + 1) * LANES] = jnp.concatenate(lses[(rb, i)] + [pad], axis=0).T


def _attention(qt, k, vt, dilation):
    rb, qb = _group_tiles(dilation)
    n_sub = SEQ // dilation
    per = qb // WIN
    prev_blk = lambda j: jnp.maximum(j * per - 1, 0)
    out, lse = pl.pallas_call(
        functools.partial(_attn_kernel, rb_count=rb, qb=qb),
        grid=(BATCH, dilation // rb, n_sub // qb),
        in_specs=[pl.BlockSpec((None, rb, ATTN_WIDTH, qb), lambda b, r, j: (b, r, 0, j)),
                  pl.BlockSpec((None, rb, qb, ATTN_WIDTH), lambda b, r, j: (b, r, j, 0)),
                  pl.BlockSpec((None, rb, WIN, ATTN_WIDTH), lambda b, r, j: (b, r, prev_blk(j), 0)),
                  pl.BlockSpec((None, rb, ATTN_WIDTH, qb), lambda b, r, j: (b, r, 0, j)),
                  pl.BlockSpec((None, rb, ATTN_WIDTH, WIN), lambda b, r, j: (b, r, 0, prev_blk(j)))],
        out_specs=[pl.BlockSpec((None, qb, rb * ATTN_WIDTH), lambda b, r, j: (b, j, r)),
                   pl.BlockSpec((None, qb, rb * LANES), lambda b, r, j: (b, j, r))],
        out_shape=[jax.ShapeDtypeStruct((BATCH, n_sub, dilation * ATTN_WIDTH), F32),
                   jax.ShapeDtypeStruct((BATCH, n_sub, dilation * LANES), F32)],
        scratch_shapes=[pltpu.VMEM((rb * per * (HEADS // 2), 2 * WIN, 2 * WIN), F32),
                        pltpu.VMEM((rb * per * (HEADS // 2), 2 * WIN, 2 * WIN), BF16)],
        compiler_params=_params(3),
        name=f"attn_d{dilation}",
    )(qt, k, k, vt, vt)
    return out.reshape(BATCH * n_sub, dilation * ATTN_WIDTH), lse.reshape(BATCH * n_sub, dilation * LANES)


def _combine_kernel(x_ref, o0_ref, o1_ref, o2_ref, l0_ref, l1_ref, l2_ref, ex_ref, w_ref, mg_ref, w1_ref, w2_ref,
                    out_ref, o_scr, l_scr):
    n_slab = ATTN_WIDTH // LANES
    for gi, (o_ref, l_ref) in enumerate(((o1_ref, l1_ref), (o2_ref, l2_ref))):
        d = ATTN_GROUPS[gi + 1][1]
        n = ROW_TILE // d
        for r in range(d):
            rows = pl.ds(r, n, stride=d)
            for c in range(n_slab):
                o_scr[gi, c, rows, :] = o_ref[:, r * ATTN_WIDTH + c * LANES: r * ATTN_WIDTH + (c + 1) * LANES]
            l_scr[gi, rows, :] = l_ref[:, r * LANES:(r + 1) * LANES]
    os = [o0_ref[...]] + [jnp.concatenate([o_scr[gi, c] for c in range(n_slab)], axis=1) for gi in range(2)]
    ls = [l0_ref[...], l_scr[0], l_scr[1]]
    m = jnp.maximum(jnp.maximum(ls[0], ls[1]), ls[2])
    es = [jnp.exp(l - m) for l in ls]
    inv = 1.0 / (es[0] + es[1] + es[2])
    mixed = jnp.zeros((ROW_TILE, ATTN_WIDTH), F32)
    for e, o in zip(es, os):
        wide = jnp.dot((e * inv).astype(BF16), ex_ref[...], preferred_element_type=F32)
        mixed = mixed + wide * o
    x_mid = x_ref[...] + jnp.dot(mixed.astype(BF16), w_ref[...], preferred_element_type=F32)
    out_ref[...] = _mlp_rows(x_mid, mg_ref, w1_ref, w2_ref)


def _combine_project_mlp(xt, outs, lses, expand, w_out, mlp_g, w1, w2, j, layer):
    t = xt.shape[0]
    assert ATTN_GROUPS[0][1] == 1
    dil = lambda w: [pl.BlockSpec((ROW_TILE // d, d * w), lambda i: (i, 0)) for _, d in ATTN_GROUPS]
    row = pl.BlockSpec((ROW_TILE, D_MODEL), lambda i: (i, 0))
    return pl.pallas_call(
        _combine_kernel,
        grid=(t // ROW_TILE,),
        in_specs=[row] + dil(ATTN_WIDTH) + dil(LANES)
                 + [_const_spec((LANES, ATTN_WIDTH)), _layer_spec((ATTN_WIDTH, D_MODEL), j)] + _mlp_specs(layer),
        out_specs=row,
        out_shape=jax.ShapeDtypeStruct(xt.shape, F32),
        scratch_shapes=[pltpu.VMEM((N_GROUPS - 1, ATTN_WIDTH // LANES, ROW_TILE, LANES), F32),
                        pltpu.VMEM((N_GROUPS - 1, ROW_TILE, LANES), F32)],
        compiler_params=_params(1),
        name="attn_combine_mlp",
    )(xt, *outs, *lses, expand, w_out, mlp_g.reshape(1, D_MODEL), w1, w2)


def _rope_tables():
    pos = jnp.arange(SEQ, dtype=F32)
    inv_freq = ROPE_THETA ** (-jnp.arange(0, HEAD_DIM, 2, dtype=F32) / HEAD_DIM)
    ang = pos[:, None] * inv_freq[None, :]
    return jnp.cos(ang), jnp.sin(ang)


def _dilated_table(tab, dilation):
    tt = _qkv_step_tokens(dilation)
    nl = tt // dilation
    return tab.reshape(SEQ // tt, nl, dilation, HALF).transpose(0, 3, 2, 1).reshape(SEQ // tt, HALF, tt)


def _attention_layer(xt, w_all, q_gain, k_gain, w_out, cos, sin, mlp_g, w1, w2, j, layer):
    x3 = xt.reshape(BATCH, SEQ, D_MODEL)
    lane_bcast = lambda v: jnp.broadcast_to(v.reshape(HEAD_DIM, 1), (HEAD_DIM, LANES))
    qg = lane_bcast(q_gain * (HEAD_DIM ** -0.5 * LOG2E))
    kg = lane_bcast(k_gain)
    head = jnp.arange(LANES)[:, None]
    lane = jnp.arange(ATTN_WIDTH)[None, :]
    expand = (lane // HEAD_DIM == head).astype(BF16)
    outs, lses = [], []
    for gi, (_, dilation) in enumerate(ATTN_GROUPS):
        qt, k, vt = _qkv_project(x3, w_all, j, gi, qg, kg, _dilated_table(cos, dilation),
                                 _dilated_table(sin, dilation), dilation)
        o, l = _attention(qt, k, vt, dilation)
        outs.append(o)
        lses.append(l)
    return _combine_project_mlp(xt, outs, lses, expand, w_out, mlp_g, w1, w2, j, layer)


def kernel(x, mixer_norm, mlp_norm, conv_w_in, conv_b_in, conv_w_dw, conv_b_dw, conv_ln_g, conv_ln_b,
           conv_w_out, conv_b_out, attn_w_in, attn_q_norm, attn_k_norm, attn_w_out, mlp_w1, mlp_w2):
    assert x.shape == (BATCH, SEQ, D_MODEL) and x.dtype == F32
    cos, sin = _rope_tables()
    w1_all, w2_all = mlp_w1.astype(BF16), mlp_w2.astype(BF16)
    conv_in_all, conv_out_all = conv_w_in.astype(BF16), conv_w_out.astype(BF16)
    attn_out_all = attn_w_out.astype(BF16)
    attn_gain = mixer_norm[1::2]
    attn_w_all = (attn_w_in * attn_gain[:, :, None]).astype(BF16)
    xt = x.reshape(BATCH * SEQ, D_MODEL)
    for i in range(DEPTH):
        j = i // 2
        if i % 2 == 0:
            x3 = _conv_layer(xt.reshape(BATCH, SEQ, D_MODEL), mixer_norm[i], conv_in_all, conv_b_in[j],
                             conv_w_dw[j], conv_b_dw[j], conv_ln_g[j], conv_ln_b[j], conv_out_all, conv_b_out[j],
                             mlp_norm[i], w1_all, w2_all, j, i)
            xt = x3.reshape(BATCH * SEQ, D_MODEL)
        else:
            xt = _attention_layer(xt, attn_w_all, attn_q_norm[j], attn_k_norm[j], attn_out_all, cos, sin,
                                  mlp_norm[i], w1_all, w2_all, j, i)
    return xt.reshape(BATCH, SEQ, D_MODEL)
```

```python
import functools

import jax
import jax.numpy as jnp
from jax import lax
from jax.experimental import pallas as pl
from jax.experimental.pallas import tpu as pltpu

D_MODEL = 1024
BATCH = 8
SEQ = 4096
DEPTH = 4
CONV_KERNEL = 31
ATTN_GROUPS = ((128, 1), (512, 4), (2048, 16))
N_GROUPS = len(ATTN_GROUPS)
HEADS = 8
HEAD_DIM = 64
HALF = HEAD_DIM // 2
ATTN_WIDTH = HEADS * HEAD_DIM
D_FF = 4 * D_MODEL
ROPE_THETA = 10000.0
EPS = 1e-6

LANES = 128
BF16_ROWS = 16
WIN = 128
V7X_VMEM_LIMIT = 56 * 1024 * 1024

ROW_TILE = 512
CONV_HALO = 32
CONV_CHUNK = 64
CONV_PAIR = CONV_CHUNK // 2
CONV_GROUP = 4
N_SLABS = D_MODEL // LANES
CONV_SLABS = N_SLABS
QKV_CHUNK = 512
ATTN_STEP_TOKENS = 2048
FF_CHUNK = 1024

F32 = jnp.float32
BF16 = jnp.bfloat16
NEG = -1e30
LN2 = 0.6931471805599453
LOG2E = 1.4426950408889634


def _params(n_axes):
    return pltpu.CompilerParams(
        dimension_semantics=("arbitrary",) * n_axes,
        vmem_limit_bytes=V7X_VMEM_LIMIT,
    )


def _const_spec(shape):
    nd = len(shape)
    return pl.BlockSpec(shape, lambda *_: (0,) * nd, pipeline_mode=pl.Buffered(1))


def _layer_spec(shape, layer):
    nd = len(shape)
    return pl.BlockSpec((None,) + tuple(shape), lambda *_: (layer,) + (0,) * nd, pipeline_mode=pl.Buffered(1))


def _rms_rows(x, g):
    return x * lax.rsqrt(jnp.mean(x * x, axis=-1, keepdims=True) + EPS) * g


def _sigmoid(x):
    return 1.0 / (1.0 + jnp.exp(-x))


def _mlp_rows(x, g_ref, w1_ref, w2_ref):
    h = _rms_rows(x, g_ref[...]).astype(BF16)
    acc = x
    for c in range(D_FF // FF_CHUNK):
        cs = slice(c * FF_CHUNK, (c + 1) * FF_CHUNK)
        z = jnp.dot(h, w1_ref[:, cs], preferred_element_type=F32)
        z = jnp.maximum(z, 0.0)
        acc = acc + jnp.dot((z * z).astype(BF16), w2_ref[cs, :], preferred_element_type=F32)
    return acc


def _mlp_specs(layer):
    return [_const_spec((1, D_MODEL)), _layer_spec((D_MODEL, D_FF), layer), _layer_spec((D_FF, D_MODEL), layer)]


def _conv_kernel(x_ref, g_ref, win_ref, bin_ref, wdw_ref, bdw_ref, lng_ref, lnb_ref, wout_ref, bout_ref,
                 mg_ref, w1_ref, w2_ref, o_ref, ubuf, pkbuf, cbuf):
    x = x_ref[...]
    h = _rms_rows(x, g_ref[...]).astype(BF16)
    a = jnp.dot(h, win_ref[...], preferred_element_type=F32) + bin_ref[...]
    u = (a[:, :D_MODEL] * _sigmoid(a[:, D_MODEL:])).astype(BF16).astype(F32)

    @pl.when(pl.program_id(1) == 0)
    def _():
        ubuf[:, 0:CONV_HALO, :] = jnp.zeros((CONV_SLABS, CONV_HALO, LANES), F32)

    n_words = CONV_HALO + ROW_TILE - CONV_PAIR
    for c in range(CONV_SLABS):
        ubuf[c, CONV_HALO:CONV_HALO + ROW_TILE, :] = u[:, c * LANES:(c + 1) * LANES]
        bits = pltpu.bitcast(ubuf[c], jnp.uint32)
        pkbuf[c] = lax.shift_right_logical(bits[:n_words], jnp.uint32(16)) | bits[CONV_PAIR:CONV_PAIR + n_words]
    first_tap = CONV_HALO - (CONV_KERNEL - 1)
    high_half = jnp.uint32(0xFFFF0000)

    def slab(c, carry):
        for base in range(0, ROW_TILE, CONV_CHUNK):
            acc_lo = jnp.broadcast_to(bdw_ref[c], (CONV_PAIR, LANES))
            acc_hi = acc_lo
            for k0 in range(0, CONV_KERNEL, CONV_GROUP):
                part = None
                for k in range(k0, min(k0 + CONV_GROUP, CONV_KERNEL)):
                    lo = base + first_tap + k
                    win = pltpu.bitcast(pkbuf[c, lo:lo + CONV_PAIR, :], BF16)
                    wk = jnp.concatenate([wdw_ref[c, k]] * (CONV_CHUNK // BF16_ROWS), axis=0)
                    part = win * wk if part is None else part + win * wk
                words = pltpu.bitcast(part, jnp.uint32)
                acc_lo = acc_lo + pltpu.bitcast(lax.shift_left(words, jnp.uint32(16)), F32)
                acc_hi = acc_hi + pltpu.bitcast(words & high_half, F32)
            cbuf[c, base:base + CONV_PAIR, :] = acc_lo
            cbuf[c, base + CONV_PAIR:base + CONV_CHUNK, :] = acc_hi
        return carry

    lax.fori_loop(0, CONV_SLABS, slab, 0)
    for c in range(CONV_SLABS):
        ubuf[c, 0:CONV_HALO, :] = ubuf[c, ROW_TILE:ROW_TILE + CONV_HALO, :]
    conv = jnp.concatenate([cbuf[c] for c in range(CONV_SLABS)], axis=1)
    mu = jnp.mean(conv, axis=-1, keepdims=True)
    xc = conv - mu
    y = xc * lax.rsqrt(jnp.mean(xc * xc, axis=-1, keepdims=True) + EPS) * lng_ref[...] + lnb_ref[...]
    y = (y * _sigmoid(y)).astype(BF16)
    x_mid = x + jnp.dot(y, wout_ref[...], preferred_element_type=F32) + bout_ref[...]
    o_ref[...] = _mlp_rows(x_mid, mg_ref, w1_ref, w2_ref)


def _conv_layer(x3, g, w_in, b_in, w_dw, b_dw, ln_g, ln_b, w_out, b_out, mlp_g, w1, w2, j, layer):
    row = pl.BlockSpec((None, ROW_TILE, D_MODEL), lambda b, j: (b, j, 0))
    vec = lambda v: v.reshape(1, -1)
    w_slabs = jnp.broadcast_to(w_dw.reshape(CONV_KERNEL, CONV_SLABS, 1, LANES).transpose(1, 0, 2, 3),
                               (CONV_SLABS, CONV_KERNEL, BF16_ROWS, LANES)).astype(BF16)
    b_slabs = b_dw.reshape(CONV_SLABS, 1, LANES)
    return pl.pallas_call(
        _conv_kernel,
        grid=(BATCH, SEQ // ROW_TILE),
        in_specs=[row, _const_spec((1, D_MODEL)), _layer_spec((D_MODEL, 2 * D_MODEL), j),
                  _const_spec((1, 2 * D_MODEL)), _const_spec((CONV_SLABS, CONV_KERNEL, BF16_ROWS, LANES)),
                  _const_spec((CONV_SLABS, 1, LANES)), _const_spec((1, D_MODEL)), _const_spec((1, D_MODEL)),
                  _layer_spec((D_MODEL, D_MODEL), j), _const_spec((1, D_MODEL))] + _mlp_specs(layer),
        out_specs=row,
        out_shape=jax.ShapeDtypeStruct(x3.shape, F32),
        scratch_shapes=[pltpu.VMEM((CONV_SLABS, ROW_TILE + CONV_HALO, LANES), F32),
                        pltpu.VMEM((CONV_SLABS, ROW_TILE + CONV_HALO - CONV_PAIR, LANES), jnp.uint32),
                        pltpu.VMEM((CONV_SLABS, ROW_TILE, LANES), F32)],
        compiler_params=_params(2),
        name="conv_layer",
    )(x3, vec(g), w_in, vec(b_in), w_slabs, b_slabs, vec(ln_g), vec(ln_b), w_out, vec(b_out), vec(mlp_g), w1, w2)


def _group_tiles(dilation):
    n_sub = SEQ // dilation
    tl = min(n_sub, ATTN_STEP_TOKENS)
    return ATTN_STEP_TOKENS // tl, tl


def _qkv_step_tokens(dilation):
    return max(4 * QKV_CHUNK, WIN * dilation)


def _qkv_kernel(*refs, dilation, tt):
    x_refs = refs[:N_SLABS]
    w_refs = refs[N_SLABS:N_SLABS + 3]
    qg_ref, kg_ref, cos_ref, sin_ref, qt_ref, k_ref, vt_ref, rs_scr = refs[N_SLABS + 3:N_SLABS + 11]
    nl = tt // dilation
    ssq = x_refs[0][...] * x_refs[0][...]
    for c in range(1, N_SLABS):
        ssq = ssq + x_refs[c][...] * x_refs[c][...]
    rs = lax.rsqrt(jnp.sum(ssq, axis=-1, keepdims=True) * (1.0 / D_MODEL) + EPS)
    rs_scr[...] = jnp.broadcast_to(rs, (tt, LANES))

    if dilation == 16:
        a_scr, rsa_scr = refs[N_SLABS + 11:]
        quarter = tt // 4
        for q in range(4):
            dst = slice(q * quarter, (q + 1) * quarter)
            for c in range(N_SLABS):
                a_scr[c, dst, :] = x_refs[c][pl.ds(q, quarter, stride=4), :]
            rsa_scr[dst, :] = rs_scr[pl.ds(q, quarter, stride=4), :]

    def residue_rows(r, lo, n):
        if dilation == 1:
            idx = pl.ds(lo, n)
            return [x_refs[c][idx, :] for c in range(N_SLABS)], rs_scr[idx, :]
        if dilation == 4:
            idx = pl.ds(r + lo * 4, n, stride=4)
            return [x_refs[c][idx, :] for c in range(N_SLABS)], rs_scr[idx, :]
        idx = pl.ds((r % 4) * (tt // 4) + r // 4 + lo * 4, n, stride=4)
        return [a_scr[c, idx, :] for c in range(N_SLABS)], rsa_scr[idx, :]

    widen = lambda t: jnp.concatenate([t] * (QKV_CHUNK // LANES), axis=1)
    gains = (widen(qg_ref[...]), widen(kg_ref[...]))
    for ck in range(tt // QKV_CHUNK):
        if nl >= QKV_CHUNK:
            per = nl // QKV_CHUNK
            pieces = [(ck // per, (ck % per) * QKV_CHUNK, QKV_CHUNK)]
        else:
            pieces = [(ck * (QKV_CHUNK // nl) + i, 0, nl) for i in range(QKV_CHUNK // nl)]
        hs = []
        for r, lo, n in pieces:
            slabs, scale = residue_rows(r, lo, n)
            hs.append(jnp.concatenate([slabs[c] * scale for c in range(N_SLABS)], axis=1).astype(BF16))
        h = jnp.concatenate(hs, axis=0) if len(hs) > 1 else hs[0]
        ys = [lax.dot_general(w_ref[...], h, (((0,), (1,)), ((), ())), preferred_element_type=F32)
              for w_ref in w_refs]
        cols = slice(ck * QKV_CHUNK, (ck + 1) * QKV_CHUNK)
        cos = cos_ref[:, cols]
        sin = sin_ref[:, cols]
        rot = []
        for s in range(2):
            g1, g2 = gains[s][:HALF], gains[s][HALF:]
            c1, s2, c2, s1 = g1 * cos, g2 * sin, g2 * cos, g1 * sin
            parts = []
            for hh in range(HEADS):
                t = ys[s][hh * HEAD_DIM:(hh + 1) * HEAD_DIM, :]
                inv = lax.rsqrt(jnp.mean(t * t, axis=0, keepdims=True) + EPS)
                t1, t2 = t[:HALF], t[HALF:]
                parts += [(t1 * c1 - t2 * s2) * inv, (t2 * c2 + t1 * s1) * inv]
            rot.append(jnp.concatenate(parts, axis=0))
        q_t = rot[0].astype(BF16)
        k_rows = rot[1].T.astype(BF16)
        v_t = ys[2].astype(BF16)
        off = 0
        for r, lo, n in pieces:
            qt_ref[r, :, lo:lo + n] = q_t[:, off:off + n]
            k_ref[r, lo:lo + n, :] = k_rows[off:off + n, :]
            vt_ref[r, :, lo:lo + n] = v_t[:, off:off + n]
            off += n


def _qkv_project(x3, w_all, j, group, qg, kg, cos_t, sin_t, dilation):
    tt = _qkv_step_tokens(dilation)
    n_sub = SEQ // dilation
    nl = tt // dilation
    slab = lambda c: pl.BlockSpec((None, tt, LANES), lambda b, j: (b, j, c))
    t_spec = pl.BlockSpec((None, dilation, ATTN_WIDTH, nl), lambda b, j: (b, 0, 0, j))
    r_spec = pl.BlockSpec((None, dilation, nl, ATTN_WIDTH), lambda b, j: (b, 0, j, 0))
    tab = pl.BlockSpec((None, HALF, tt), lambda b, j: (j, 0, 0))
    scratch = [pltpu.VMEM((tt, LANES), F32)]
    if dilation == 16:
        scratch += [pltpu.VMEM((N_SLABS, tt, LANES), F32), pltpu.VMEM((tt, LANES), F32)]
    return pl.pallas_call(
        functools.partial(_qkv_kernel, dilation=dilation, tt=tt),
        grid=(BATCH, SEQ // tt),
        in_specs=[slab(c) for c in range(N_SLABS)]
                 + [pl.BlockSpec((None, D_MODEL, ATTN_WIDTH), lambda b, i, s=s: (j, 0, s * N_GROUPS + group),
                                 pipeline_mode=pl.Buffered(1)) for s in range(3)]
                 + [_const_spec((HEAD_DIM, LANES)), _const_spec((HEAD_DIM, LANES)), tab, tab],
        out_specs=[t_spec, r_spec, t_spec],
        out_shape=[jax.ShapeDtypeStruct((BATCH, dilation, ATTN_WIDTH, n_sub), BF16),
                   jax.ShapeDtypeStruct((BATCH, dilation, n_sub, ATTN_WIDTH), BF16),
                   jax.ShapeDtypeStruct((BATCH, dilation, ATTN_WIDTH, n_sub), BF16)],
        scratch_shapes=scratch,
        compiler_params=_params(2),
        name=f"qkv_d{dilation}",
    )(*([x3] * N_SLABS), w_all, w_all, w_all, qg, kg, cos_t, sin_t)


def _attn_kernel(qt_ref, kc_ref, kp_ref, vc_ref, vp_ref, o_ref, lse_ref, s_scr, p_scr, *, rb_count, qb):
    kj = lax.broadcasted_iota(jnp.int32, (WIN, WIN), 0)
    qi = lax.broadcasted_iota(jnp.int32, (WIN, WIN), 1)
    eye = jnp.where(kj == qi, 1.0, 0.0).astype(BF16)
    two = lambda m: jnp.concatenate([m, m], axis=1).astype(BF16)
    mask_prev = two(jnp.where(kj >= qi, 0.0, NEG))
    mask_cur = two(jnp.where(kj <= qi, 0.0, NEG))
    mask_none = two(jnp.full((WIN, WIN), NEG, F32))
    mask_head = jnp.where(pl.program_id(2) > 0, mask_prev, mask_none)
    zeros = jnp.zeros((HEAD_DIM, WIN), BF16)
    ones = jnp.ones((BF16_ROWS, 2 * WIN), BF16)
    pad = jnp.zeros((LANES - HEADS, WIN), F32)
    units = [(rb, i, p) for rb in range(rb_count) for i in range(qb // WIN) for p in range(HEADS // 2)]

    def slices(i, p):
        return (slice(i * WIN, (i + 1) * WIN), slice((i - 1) * WIN, i * WIN),
                slice(p * 2 * HEAD_DIM, (p + 1) * 2 * HEAD_DIM))

    for n, (rb, i, p) in enumerate(units):
        cur, prev, cols = slices(i, p)
        k_prev = kp_ref[rb, :, cols] if i == 0 else kc_ref[rb, prev, cols]
        q2 = qt_ref[rb, cols, cur]
        qbd = jnp.concatenate(
            [jnp.concatenate([q2[:HEAD_DIM], zeros], axis=1),
             jnp.concatenate([zeros, q2[HEAD_DIM:]], axis=1)], axis=0)
        m_prev = mask_head if i == 0 else mask_prev
        s_scr[n, 0:WIN, :] = jnp.dot(jnp.concatenate([k_prev, eye], axis=1),
                                     jnp.concatenate([qbd, m_prev], axis=0), preferred_element_type=F32)
        s_scr[n, WIN:2 * WIN, :] = jnp.dot(jnp.concatenate([kc_ref[rb, cur, cols], eye], axis=1),
                                           jnp.concatenate([qbd, mask_cur], axis=0), preferred_element_type=F32)

    stats = []
    for n in range(len(units)):
        s = s_scr[n]
        m = jnp.max(s, axis=0, keepdims=True)
        e = jnp.exp2(s - m)
        p_scr[n] = e.astype(BF16)
        stats.append(m)

    outs, lses = {}, {}
    for n, (rb, i, p) in enumerate(units):
        cur, prev, cols = slices(i, p)
        v_prev = vp_ref[rb, cols, :] if i == 0 else vc_ref[rb, cols, prev]
        vv = jnp.concatenate([v_prev, vc_ref[rb, cols, cur]], axis=1)
        o2 = jnp.dot(jnp.concatenate([vv, ones], axis=0), p_scr[n], preferred_element_type=F32)
        m, den = stats[n], o2[2 * HEAD_DIM:2 * HEAD_DIM + 1, :]
        inv = 1.0 / den
        outs.setdefault((rb, i), []).extend(
            [o2[:HEAD_DIM, :WIN] * inv[:, :WIN], o2[HEAD_DIM:2 * HEAD_DIM, WIN:] * inv[:, WIN:]])
        lse = m * LN2 + jnp.log(den)
        lses.setdefault((rb, i), []).extend([lse[:, :WIN], lse[:, WIN:]])
    for (rb, i), pieces in outs.items():
        cur = slice(i * WIN, (i + 1) * WIN)
        o_ref[cur, rb * ATTN_WIDTH:(rb + 1) * ATTN_WIDTH] = jnp.concatenate(pieces, axis=0).T
        lse_ref[cur, rb * LANES:(rb + 1) * LANES] = jnp.concatenate(lses[(rb, i)] + [pad], axis=0).T


def _attention(qt, k, vt, dilation):
    rb, qb = _group_tiles(dilation)
    n_sub = SEQ // dilation
    per = qb // WIN
    prev_blk = lambda j: jnp.maximum(j * per - 1, 0)
    out, lse = pl.pallas_call(
        functools.partial(_attn_kernel, rb_count=rb, qb=qb),
        grid=(BATCH, dilation // rb, n_sub // qb),
        in_specs=[pl.BlockSpec((None, rb, ATTN_WIDTH, qb), lambda b, r, j: (b, r, 0, j)),
                  pl.BlockSpec((None, rb, qb, ATTN_WIDTH), lambda b, r, j: (b, r, j, 0)),
                  pl.BlockSpec((None, rb, WIN, ATTN_WIDTH), lambda b, r, j: (b, r, prev_blk(j), 0)),
                  pl.BlockSpec((None, rb, ATTN_WIDTH, qb), lambda b, r, j: (b, r, 0, j)),
                  pl.BlockSpec((None, rb, ATTN_WIDTH, WIN), lambda b, r, j: (b, r, 0, prev_blk(j)))],
        out_specs=[pl.BlockSpec((None, qb, rb * ATTN_WIDTH), lambda b, r, j: (b, j, r)),
                   pl.BlockSpec((None, qb, rb * LANES), lambda b, r, j: (b, j, r))],
        out_shape=[jax.ShapeDtypeStruct((BATCH, n_sub, dilation * ATTN_WIDTH), F32),
                   jax.ShapeDtypeStruct((BATCH, n_sub, dilation * LANES), F32)],
        scratch_shapes=[pltpu.VMEM((rb * per * (HEADS // 2), 2 * WIN, 2 * WIN), F32),
                        pltpu.VMEM((rb * per * (HEADS // 2), 2 * WIN, 2 * WIN), BF16)],
        compiler_params=_params(3),
        name=f"attn_d{dilation}",
    )(qt, k, k, vt, vt)
    return out.reshape(BATCH * n_sub, dilation * ATTN_WIDTH), lse.reshape(BATCH * n_sub, dilation * LANES)


def _combine_kernel(x_ref, o0_ref, o1_ref, o2_ref, l0_ref, l1_ref, l2_ref, ex_ref, w_ref, mg_ref, w1_ref, w2_ref,
                    out_ref, o_scr, l_scr):
    n_slab = ATTN_WIDTH // LANES
    for gi, (o_ref, l_ref) in enumerate(((o1_ref, l1_ref), (o2_ref, l2_ref))):
        d = ATTN_GROUPS[gi + 1][1]
        n = ROW_TILE // d
        for r in range(d):
            rows = pl.ds(r, n, stride=d)
            for c in range(n_slab):
                o_scr[gi, c, rows, :] = o_ref[:, r * ATTN_WIDTH + c * LANES: r * ATTN_WIDTH + (c + 1) * LANES]
            l_scr[gi, rows, :] = l_ref[:, r * LANES:(r + 1) * LANES]
    os = [o0_ref[...]] + [jnp.concatenate([o_scr[gi, c] for c in range(n_slab)], axis=1) for gi in range(2)]
    ls = [l0_ref[...], l_scr[0], l_scr[1]]
    m = jnp.maximum(jnp.maximum(ls[0], ls[1]), ls[2])
    es = [jnp.exp(l - m) for l in ls]
    inv = 1.0 / (es[0] + es[1] + es[2])
    mixed = jnp.zeros((ROW_TILE, ATTN_WIDTH), F32)
    for e, o in zip(es, os):
        wide = jnp.dot((e * inv).astype(BF16), ex_ref[...], preferred_element_type=F32)
        mixed = mixed + wide * o
    x_mid = x_ref[...] + jnp.dot(mixed.astype(BF16), w_ref[...], preferred_element_type=F32)
    out_ref[...] = _mlp_rows(x_mid, mg_ref, w1_ref, w2_ref)


def _combine_project_mlp(xt, outs, lses, expand, w_out, mlp_g, w1, w2, j, layer):
    t = xt.shape[0]
    assert ATTN_GROUPS[0][1] == 1
    dil = lambda w: [pl.BlockSpec((ROW_TILE // d, d * w), lambda i: (i, 0)) for _, d in ATTN_GROUPS]
    row = pl.BlockSpec((ROW_TILE, D_MODEL), lambda i: (i, 0))
    return pl.pallas_call(
        _combine_kernel,
        grid=(t // ROW_TILE,),
        in_specs=[row] + dil(ATTN_WIDTH) + dil(LANES)
                 + [_const_spec((LANES, ATTN_WIDTH)), _layer_spec((ATTN_WIDTH, D_MODEL), j)] + _mlp_specs(layer),
        out_specs=row,
        out_shape=jax.ShapeDtypeStruct(xt.shape, F32),
        scratch_shapes=[pltpu.VMEM((N_GROUPS - 1, ATTN_WIDTH // LANES, ROW_TILE, LANES), F32),
                        pltpu.VMEM((N_GROUPS - 1, ROW_TILE, LANES), F32)],
        compiler_params=_params(1),
        name="attn_combine_mlp",
    )(xt, *outs, *lses, expand, w_out, mlp_g.reshape(1, D_MODEL), w1, w2)


def _rope_tables():
    pos = jnp.arange(SEQ, dtype=F32)
    inv_freq = ROPE_THETA ** (-jnp.arange(0, HEAD_DIM, 2, dtype=F32) / HEAD_DIM)
    ang = pos[:, None] * inv_freq[None, :]
    return jnp.cos(ang), jnp.sin(ang)


def _dilated_table(tab, dilation):
    tt = _qkv_step_tokens(dilation)
    nl = tt // dilation
    return tab.reshape(SEQ // tt, nl, dilation, HALF).transpose(0, 3, 2, 1).reshape(SEQ // tt, HALF, tt)


def _attention_layer(xt, w_all, q_gain, k_gain, w_out, cos, sin, mlp_g, w1, w2, j, layer):
    x3 = xt.reshape(BATCH, SEQ, D_MODEL)
    lane_bcast = lambda v: jnp.broadcast_to(v.reshape(HEAD_DIM, 1), (HEAD_DIM, LANES))
    qg = lane_bcast(q_gain * (HEAD_DIM ** -0.5 * LOG2E))
    kg = lane_bcast(k_gain)
    head = jnp.arange(LANES)[:, None]
    lane = jnp.arange(ATTN_WIDTH)[None, :]
    expand = (lane // HEAD_DIM == head).astype(BF16)
    outs, lses = [], []
    for gi, (_, dilation) in enumerate(ATTN_GROUPS):
        qt, k, vt = _qkv_project(x3, w_all, j, gi, qg, kg, _dilated_table(cos, dilation),
                                 _dilated_table(sin, dilation), dilation)
        o, l = _attention(qt, k, vt, dilation)
        outs.append(o)
        lses.append(l)
    return _combine_project_mlp(xt, outs, lses, expand, w_out, mlp_g, w1, w2, j, layer)


def kernel(x, mixer_norm, mlp_norm, conv_w_in, conv_b_in, conv_w_dw, conv_b_dw, conv_ln_g, conv_ln_b,
           conv_w_out, conv_b_out, attn_w_in, attn_q_norm, attn_k_norm, attn_w_out, mlp_w1, mlp_w2):
    assert x.shape == (BATCH, SEQ, D_MODEL) and x.dtype == F32
    cos, sin = _rope_tables()
    w1_all, w2_all = mlp_w1.astype(BF16), mlp_w2.astype(BF16)
    conv_in_all, conv_out_all = conv_w_in.astype(BF16), conv_w_out.astype(BF16)
    attn_out_all = attn_w_out.astype(BF16)
    attn_gain = mixer_norm[1::2]
    attn_w_all = (attn_w_in * attn_gain[:, :, None]).astype(BF16)
    xt = x.reshape(BATCH * SEQ, D_MODEL)
    for i in range(DEPTH):
        j = i // 2
        if i % 2 == 0:
            x3 = _conv_layer(xt.reshape(BATCH, SEQ, D_MODEL), mixer_norm[i], conv_in_all, conv_b_in[j],
                             conv_w_dw[j], conv_b_dw[j], conv_ln_g[j], conv_ln_b[j], conv_out_all, conv_b_out[j],
                             mlp_norm[i], w1_all, w2_all, j, i)
            xt = x3.reshape(BATCH * SEQ, D_MODEL)
        else:
            xt = _attention_layer(xt, attn_w_all, attn_q_norm[j], attn_k_norm[j], attn_out_all, cos, sin,
                                  mlp_norm[i], w1_all, w2_all, j, i)
    return xt.reshape(BATCH, SEQ, D_MODEL)
```

```python
import functools

import jax
import jax.numpy as jnp
from jax import lax
from jax.experimental import pallas as pl
from jax.experimental.pallas import tpu as pltpu

D_MODEL = 1024
BATCH = 8
SEQ = 4096
DEPTH = 4
CONV_KERNEL = 31
ATTN_GROUPS = ((128, 1), (512, 4), (2048, 16))
N_GROUPS = len(ATTN_GROUPS)
HEADS = 8
HEAD_DIM = 64
HALF = HEAD_DIM // 2
ATTN_WIDTH = HEADS * HEAD_DIM
D_FF = 4 * D_MODEL
ROPE_THETA = 10000.0
EPS = 1e-6

LANES = 128
BF16_ROWS = 16
WIN = 128
V7X_VMEM_LIMIT = 56 * 1024 * 1024

ROW_TILE = 512
CONV_HALO = 32
CONV_CHUNK = 64
CONV_PAIR = CONV_CHUNK // 2
CONV_GROUP = 4
N_SLABS = D_MODEL // LANES
CONV_SLABS = N_SLABS
QKV_CHUNK = 512
ATTN_STEP_TOKENS = 2048
FF_CHUNK = 1024

F32 = jnp.float32
BF16 = jnp.bfloat16
NEG = -1e30
LN2 = 0.6931471805599453
LOG2E = 1.4426950408889634


def _params(n_axes):
    return pltpu.CompilerParams(
        dimension_semantics=("arbitrary",) * n_axes,
        vmem_limit_bytes=V7X_VMEM_LIMIT,
    )


def _const_spec(shape):
    nd = len(shape)
    return pl.BlockSpec(shape, lambda *_: (0,) * nd, pipeline_mode=pl.Buffered(1))


def _layer_spec(shape, layer):
    nd = len(shape)
    return pl.BlockSpec((None,) + tuple(shape), lambda *_: (layer,) + (0,) * nd, pipeline_mode=pl.Buffered(1))


def _rms_rows(x, g):
    return x * lax.rsqrt(jnp.mean(x * x, axis=-1, keepdims=True) + EPS) * g


def _sigmoid(x):
    return 1.0 / (1.0 + jnp.exp(-x))


def _mlp_rows(x, g_ref, w1_ref, w2_ref):
    h = _rms_rows(x, g_ref[...]).astype(BF16)
    acc = x
    for c in range(D_FF // FF_CHUNK):
        cs = slice(c * FF_CHUNK, (c + 1) * FF_CHUNK)
        z = jnp.dot(h, w1_ref[:, cs], preferred_element_type=F32)
        z = jnp.maximum(z, 0.0)
        acc = acc + jnp.dot((z * z).astype(BF16), w2_ref[cs, :], preferred_element_type=F32)
    return acc


def _mlp_specs(layer):
    return [_const_spec((1, D_MODEL)), _layer_spec((D_MODEL, D_FF), layer), _layer_spec((D_FF, D_MODEL), layer)]


def _conv_kernel(x_ref, g_ref, win_ref, bin_ref, wdw_ref, bdw_ref, lng_ref, lnb_ref, wout_ref, bout_ref,
                 mg_ref, w1_ref, w2_ref, o_ref, ubuf, pkbuf, cbuf):
    x = x_ref[...]
    h = _rms_rows(x, g_ref[...]).astype(BF16)
    a = jnp.dot(h, win_ref[...], preferred_element_type=F32) + bin_ref[...]
    u = (a[:, :D_MODEL] * _sigmoid(a[:, D_MODEL:])).astype(BF16).astype(F32)

    @pl.when(pl.program_id(1) == 0)
    def _():
        ubuf[:, 0:CONV_HALO, :] = jnp.zeros((CONV_SLABS, CONV_HALO, LANES), F32)

    n_words = CONV_HALO + ROW_TILE - CONV_PAIR
    for c in range(CONV_SLABS):
        ubuf[c, CONV_HALO:CONV_HALO + ROW_TILE, :] = u[:, c * LANES:(c + 1) * LANES]
        bits = pltpu.bitcast(ubuf[c], jnp.uint32)
        pkbuf[c] = lax.shift_right_logical(bits[:n_words], jnp.uint32(16)) | bits[CONV_PAIR:CONV_PAIR + n_words]
    first_tap = CONV_HALO - (CONV_KERNEL - 1)
    high_half = jnp.uint32(0xFFFF0000)

    def slab(c, carry):
        for base in range(0, ROW_TILE, CONV_CHUNK):
            acc_lo = jnp.broadcast_to(bdw_ref[c], (CONV_PAIR, LANES))
            acc_hi = acc_lo
            for k0 in range(0, CONV_KERNEL, CONV_GROUP):
                part = None
                for k in range(k0, min(k0 + CONV_GROUP, CONV_KERNEL)):
                    lo = base + first_tap + k
                    win = pltpu.bitcast(pkbuf[c, lo:lo + CONV_PAIR, :], BF16)
                    wk = jnp.concatenate([wdw_ref[c, k]] * (CONV_CHUNK // BF16_ROWS), axis=0)
                    part = win * wk if part is None else part + win * wk
                words = pltpu.bitcast(part, jnp.uint32)
                acc_lo = acc_lo + pltpu.bitcast(lax.shift_left(words, jnp.uint32(16)), F32)
                acc_hi = acc_hi + pltpu.bitcast(words & high_half, F32)
            cbuf[c, base:base + CONV_PAIR, :] = acc_lo
            cbuf[c, base + CONV_PAIR:base + CONV_CHUNK, :] = acc_hi
        return carry

    lax.fori_loop(0, CONV_SLABS, slab, 0)
    for c in range(CONV_SLABS):
        ubuf[c, 0:CONV_HALO, :] = ubuf[c, ROW_TILE:ROW_TILE + CONV_HALO, :]
    conv = jnp.concatenate([cbuf[c] for c in range(CONV_SLABS)], axis=1)
    mu = jnp.mean(conv, axis=-1, keepdims=True)
    xc = conv - mu
    y = xc * lax.rsqrt(jnp.mean(xc * xc, axis=-1, keepdims=True) + EPS) * lng_ref[...] + lnb_ref[...]
    y = (y * _sigmoid(y)).astype(BF16)
    x_mid = x + jnp.dot(y, wout_ref[...], preferred_element_type=F32) + bout_ref[...]
    o_ref[...] = _mlp_rows(x_mid, mg_ref, w1_ref, w2_ref)


def _conv_layer(x3, g, w_in, b_in, w_dw, b_dw, ln_g, ln_b, w_out, b_out, mlp_g, w1, w2, j, layer):
    row = pl.BlockSpec((None, ROW_TILE, D_MODEL), lambda b, j: (b, j, 0))
    vec = lambda v: v.reshape(1, -1)
    w_slabs = jnp.broadcast_to(w_dw.reshape(CONV_KERNEL, CONV_SLABS, 1, LANES).transpose(1, 0, 2, 3),
                               (CONV_SLABS, CONV_KERNEL, BF16_ROWS, LANES)).astype(BF16)
    b_slabs = b_dw.reshape(CONV_SLABS, 1, LANES)
    return pl.pallas_call(
        _conv_kernel,
        grid=(BATCH, SEQ // ROW_TILE),
        in_specs=[row, _const_spec((1, D_MODEL)), _layer_spec((D_MODEL, 2 * D_MODEL), j),
                  _const_spec((1, 2 * D_MODEL)), _const_spec((CONV_SLABS, CONV_KERNEL, BF16_ROWS, LANES)),
                  _const_spec((CONV_SLABS, 1, LANES)), _const_spec((1, D_MODEL)), _const_spec((1, D_MODEL)),
                  _layer_spec((D_MODEL, D_MODEL), j), _const_spec((1, D_MODEL))] + _mlp_specs(layer),
        out_specs=row,
        out_shape=jax.ShapeDtypeStruct(x3.shape, F32),
        scratch_shapes=[pltpu.VMEM((CONV_SLABS, ROW_TILE + CONV_HALO, LANES), F32),
                        pltpu.VMEM((CONV_SLABS, ROW_TILE + CONV_HALO - CONV_PAIR, LANES), jnp.uint32),
                        pltpu.VMEM((CONV_SLABS, ROW_TILE, LANES), F32)],
        compiler_params=_params(2),
        name="conv_layer",
    )(x3, vec(g), w_in, vec(b_in), w_slabs, b_slabs, vec(ln_g), vec(ln_b), w_out, vec(b_out), vec(mlp_g), w1, w2)


def _group_tiles(dilation):
    n_sub = SEQ // dilation
    tl = min(n_sub, ATTN_STEP_TOKENS)
    return ATTN_STEP_TOKENS // tl, tl


def _qkv_step_tokens(dilation):
    return max(4 * QKV_CHUNK, WIN * dilation)


def _qkv_kernel(*refs, dilation, tt):
    x_refs = refs[:N_SLABS]
    w_refs = refs[N_SLABS:N_SLABS + 3]
    qg_ref, kg_ref, cos_ref, sin_ref, qt_ref, k_ref, vt_ref, rs_scr = refs[N_SLABS + 3:N_SLABS + 11]
    nl = tt // dilation
    ssq = x_refs[0][...] * x_refs[0][...]
    for c in range(1, N_SLABS):
        ssq = ssq + x_refs[c][...] * x_refs[c][...]
    rs = lax.rsqrt(jnp.sum(ssq, axis=-1, keepdims=True) * (1.0 / D_MODEL) + EPS)
    rs_scr[...] = jnp.broadcast_to(rs, (tt, LANES))

    if dilation == 16:
        a_scr, rsa_scr = refs[N_SLABS + 11:]
        quarter = tt // 4
        for q in range(4):
            dst = slice(q * quarter, (q + 1) * quarter)
            for c in range(N_SLABS):
                a_scr[c, dst, :] = x_refs[c][pl.ds(q, quarter, stride=4), :]
            rsa_scr[dst, :] = rs_scr[pl.ds(q, quarter, stride=4), :]

    def residue_rows(r, lo, n):
        if dilation == 1:
            idx = pl.ds(lo, n)
            return [x_refs[c][idx, :] for c in range(N_SLABS)], rs_scr[idx, :]
        if dilation == 4:
            idx = pl.ds(r + lo * 4, n, stride=4)
            return [x_refs[c][idx, :] for c in range(N_SLABS)], rs_scr[idx, :]
        idx = pl.ds((r % 4) * (tt // 4) + r // 4 + lo * 4, n, stride=4)
        return [a_scr[c, idx, :] for c in range(N_SLABS)], rsa_scr[idx, :]

    widen = lambda t: jnp.concatenate([t] * (QKV_CHUNK // LANES), axis=1)
    gains = (widen(qg_ref[...]), widen(kg_ref[...]))
    for ck in range(tt // QKV_CHUNK):
        if nl >= QKV_CHUNK:
            per = nl // QKV_CHUNK
            pieces = [(ck // per, (ck % per) * QKV_CHUNK, QKV_CHUNK)]
        else:
            pieces = [(ck * (QKV_CHUNK // nl) + i, 0, nl) for i in range(QKV_CHUNK // nl)]
        hs = []
        for r, lo, n in pieces:
            slabs, scale = residue_rows(r, lo, n)
            hs.append(jnp.concatenate([slabs[c] * scale for c in range(N_SLABS)], axis=1).astype(BF16))
        h = jnp.concatenate(hs, axis=0) if len(hs) > 1 else hs[0]
        ys = [lax.dot_general(w_ref[...], h, (((0,), (1,)), ((), ())), preferred_element_type=F32)
              for w_ref in w_refs]
        cols = slice(ck * QKV_CHUNK, (ck + 1) * QKV_CHUNK)
        cos = cos_ref[:, cols]
        sin = sin_ref[:, cols]
        rot = []
        for s in range(2):
            g1, g2 = gains[s][:HALF], gains[s][HALF:]
            c1, s2, c2, s1 = g1 * cos, g2 * sin, g2 * cos, g1 * sin
            parts = []
            for hh in range(HEADS):
                t = ys[s][hh * HEAD_DIM:(hh + 1) * HEAD_DIM, :]
                inv = lax.rsqrt(jnp.mean(t * t, axis=0, keepdims=True) + EPS)
                t1, t2 = t[:HALF], t[HALF:]
                parts += [(t1 * c1 - t2 * s2) * inv, (t2 * c2 + t1 * s1) * inv]
            rot.append(jnp.concatenate(parts, axis=0))
        q_t = rot[0].astype(BF16)
        k_rows = rot[1].T.astype(BF16)
        v_t = ys[2].astype(BF16)
        off = 0
        for r, lo, n in pieces:
            qt_ref[r, :, lo:lo + n] = q_t[:, off:off + n]
            k_ref[r, lo:lo + n, :] = k_rows[off:off + n, :]
            vt_ref[r, :, lo:lo + n] = v_t[:, off:off + n]
            off += n


def _qkv_project(x3, w_all, j, group, qg, kg, cos_t, sin_t, dilation):
    tt = _qkv_step_tokens(dilation)
    n_sub = SEQ // dilation
    nl = tt // dilation
    slab = lambda c: pl.BlockSpec((None, tt, LANES), lambda b, j: (b, j, c))
    t_spec = pl.BlockSpec((None, dilation, ATTN_WIDTH, nl), lambda b, j: (b, 0, 0, j))
    r_spec = pl.BlockSpec((None, dilation, nl, ATTN_WIDTH), lambda b, j: (b, 0, j, 0))
    tab = pl.BlockSpec((None, HALF, tt), lambda b, j: (j, 0, 0))
    scratch = [pltpu.VMEM((tt, LANES), F32)]
    if dilation == 16:
        scratch += [pltpu.VMEM((N_SLABS, tt, LANES), F32), pltpu.VMEM((tt, LANES), F32)]
    return pl.pallas_call(
        functools.partial(_qkv_kernel, dilation=dilation, tt=tt),
        grid=(BATCH, SEQ // tt),
        in_specs=[slab(c) for c in range(N_SLABS)]
                 + [pl.BlockSpec((None, D_MODEL, ATTN_WIDTH), lambda b, i, s=s: (j, 0, s * N_GROUPS + group),
                                 pipeline_mode=pl.Buffered(1)) for s in range(3)]
                 + [_const_spec((HEAD_DIM, LANES)), _const_spec((HEAD_DIM, LANES)), tab, tab],
        out_specs=[t_spec, r_spec, t_spec],
        out_shape=[jax.ShapeDtypeStruct((BATCH, dilation, ATTN_WIDTH, n_sub), BF16),
                   jax.ShapeDtypeStruct((BATCH, dilation, n_sub, ATTN_WIDTH), BF16),
                   jax.ShapeDtypeStruct((BATCH, dilation, ATTN_WIDTH, n_sub), BF16)],
        scratch_shapes=scratch,
        compiler_params=_params(2),
        name=f"qkv_d{dilation}",
    )(*([x3] * N_SLABS), w_all, w_all, w_all, qg, kg, cos_t, sin_t)


def _attn_kernel(qt_ref, kc_ref, kp_ref, vc_ref, vp_ref, o_ref, lse_ref, s_scr, p_scr, *, rb_count, qb,
                 whole_sequence):
    kj = lax.broadcasted_iota(jnp.int32, (WIN, WIN), 0)
    qi = lax.broadcasted_iota(jnp.int32, (WIN, WIN), 1)
    eye = jnp.where(kj == qi, 1.0, 0.0).astype(BF16)
    two = lambda m: jnp.concatenate([m, m], axis=1).astype(BF16)
    mask_prev = two(jnp.where(kj >= qi, 0.0, NEG))
    mask_cur = two(jnp.where(kj <= qi, 0.0, NEG))
    mask_none = two(jnp.full((WIN, WIN), NEG, F32))
    mask_head = jnp.where(pl.program_id(2) > 0, mask_prev, mask_none)
    zeros = jnp.zeros((HEAD_DIM, WIN), BF16)
    ones = jnp.ones((BF16_ROWS, 2 * WIN), BF16)
    pad = jnp.zeros((LANES - HEADS, WIN), F32)
    units = [(rb, i, p) for rb in range(rb_count) for i in range(qb // WIN) for p in range(HEADS // 2)]

    def slices(i, p):
        return (slice(i * WIN, (i + 1) * WIN), slice((i - 1) * WIN, i * WIN),
                slice(p * 2 * HEAD_DIM, (p + 1) * 2 * HEAD_DIM))

    def starts_sequence(i):
        return whole_sequence and i == 0

    def key_rows(i):
        return slice(WIN, 2 * WIN) if starts_sequence(i) else slice(0, 2 * WIN)

    for n, (rb, i, p) in enumerate(units):
        cur, prev, cols = slices(i, p)
        k_prev = kp_ref[rb, :, cols] if i == 0 else kc_ref[rb, prev, cols]
        q2 = qt_ref[rb, cols, cur]
        qbd = jnp.concatenate(
            [jnp.concatenate([q2[:HEAD_DIM], zeros], axis=1),
             jnp.concatenate([zeros, q2[HEAD_DIM:]], axis=1)], axis=0)
        if not starts_sequence(i):
            m_prev = mask_head if i == 0 else mask_prev
            s_scr[n, 0:WIN, :] = jnp.dot(jnp.concatenate([k_prev, eye], axis=1),
                                         jnp.concatenate([qbd, m_prev], axis=0), preferred_element_type=F32)
        s_scr[n, WIN:2 * WIN, :] = jnp.dot(jnp.concatenate([kc_ref[rb, cur, cols], eye], axis=1),
                                           jnp.concatenate([qbd, mask_cur], axis=0), preferred_element_type=F32)

    stats = []
    for n, (rb, i, p) in enumerate(units):
        keys = key_rows(i)
        s = s_scr[n, keys, :]
        m = jnp.max(s, axis=0, keepdims=True)
        e = jnp.exp2(s - m)
        p_scr[n, keys, :] = e.astype(BF16)
        stats.append(m)

    outs, lses = {}, {}
    for n, (rb, i, p) in enumerate(units):
        cur, prev, cols = slices(i, p)
        if starts_sequence(i):
            vv = vc_ref[rb, cols, cur]
        else:
            v_prev = vp_ref[rb, cols, :] if i == 0 else vc_ref[rb, cols, prev]
            vv = jnp.concatenate([v_prev, vc_ref[rb, cols, cur]], axis=1)
        o2 = jnp.dot(jnp.concatenate([vv, ones[:, :vv.shape[1]]], axis=0), p_scr[n, key_rows(i), :],
                     preferred_element_type=F32)
        m, den = stats[n], o2[2 * HEAD_DIM:2 * HEAD_DIM + 1, :]
        inv = 1.0 / den
        outs.setdefault((rb, i), []).extend(
            [o2[:HEAD_DIM, :WIN] * inv[:, :WIN], o2[HEAD_DIM:2 * HEAD_DIM, WIN:] * inv[:, WIN:]])
        lse = m * LN2 + jnp.log(den)
        lses.setdefault((rb, i), []).extend([lse[:, :WIN], lse[:, WIN:]])
    for (rb, i), pieces in outs.items():
        cur = slice(i * WIN, (i + 1) * WIN)
        o_ref[cur, rb * ATTN_WIDTH:(rb + 1) * ATTN_WIDTH] = jnp.concatenate(pieces, axis=0).T
        lse_ref[cur, rb * LANES:(rb + 1) * LANES] = jnp.concatenate(lses[(rb, i)] + [pad], axis=0).T


def _attention(qt, k, vt, dilation):
    rb, qb = _group_tiles(dilation)
    n_sub = SEQ // dilation
    per = qb // WIN
    prev_blk = lambda j: jnp.maximum(j * per - 1, 0)
    out, lse = pl.pallas_call(
        functools.partial(_attn_kernel, rb_count=rb, qb=qb, whole_sequence=(qb == n_sub)),
        grid=(BATCH, dilation // rb, n_sub // qb),
        in_specs=[pl.BlockSpec((None, rb, ATTN_WIDTH, qb), lambda b, r, j: (b, r, 0, j)),
                  pl.BlockSpec((None, rb, qb, ATTN_WIDTH), lambda b, r, j: (b, r, j, 0)),
                  pl.BlockSpec((None, rb, WIN, ATTN_WIDTH), lambda b, r, j: (b, r, prev_blk(j), 0)),
                  pl.BlockSpec((None, rb, ATTN_WIDTH, qb), lambda b, r, j: (b, r, 0, j)),
                  pl.BlockSpec((None, rb, ATTN_WIDTH, WIN), lambda b, r, j: (b, r, 0, prev_blk(j)))],
        out_specs=[pl.BlockSpec((None, qb, rb * ATTN_WIDTH), lambda b, r, j: (b, j, r)),
                   pl.BlockSpec((None, qb, rb * LANES), lambda b, r, j: (b, j, r))],
        out_shape=[jax.ShapeDtypeStruct((BATCH, n_sub, dilation * ATTN_WIDTH), F32),
                   jax.ShapeDtypeStruct((BATCH, n_sub, dilation * LANES), F32)],
        scratch_shapes=[pltpu.VMEM((rb * per * (HEADS // 2), 2 * WIN, 2 * WIN), F32),
                        pltpu.VMEM((rb * per * (HEADS // 2), 2 * WIN, 2 * WIN), BF16)],
        compiler_params=_params(3),
        name=f"attn_d{dilation}",
    )(qt, k, k, vt, vt)
    return out.reshape(BATCH * n_sub, dilation * ATTN_WIDTH), lse.reshape(BATCH * n_sub, dilation * LANES)


def _combine_kernel(x_ref, o0_ref, o1_ref, o2_ref, l0_ref, l1_ref, l2_ref, ex_ref, w_ref, mg_ref, w1_ref, w2_ref,
                    out_ref, o_scr, l_scr):
    n_slab = ATTN_WIDTH // LANES
    for gi, (o_ref, l_ref) in enumerate(((o1_ref, l1_ref), (o2_ref, l2_ref))):
        d = ATTN_GROUPS[gi + 1][1]
        n = ROW_TILE // d
        for r in range(d):
            rows = pl.ds(r, n, stride=d)
            for c in range(n_slab):
                o_scr[gi, c, rows, :] = o_ref[:, r * ATTN_WIDTH + c * LANES: r * ATTN_WIDTH + (c + 1) * LANES]
            l_scr[gi, rows, :] = l_ref[:, r * LANES:(r + 1) * LANES]
    os = [o0_ref[...]] + [jnp.concatenate([o_scr[gi, c] for c in range(n_slab)], axis=1) for gi in range(2)]
    ls = [l0_ref[...], l_scr[0], l_scr[1]]
    m = jnp.maximum(jnp.maximum(ls[0], ls[1]), ls[2])
    es = [jnp.exp(l - m) for l in ls]
    inv = 1.0 / (es[0] + es[1] + es[2])
    mixed = jnp.zeros((ROW_TILE, ATTN_WIDTH), F32)
    for e, o in zip(es, os):
        wide = jnp.dot((e * inv).astype(BF16), ex_ref[...], preferred_element_type=F32)
        mixed = mixed + wide * o
    x_mid = x_ref[...] + jnp.dot(mixed.astype(BF16), w_ref[...], preferred_element_type=F32)
    out_ref[...] = _mlp_rows(x_mid, mg_ref, w1_ref, w2_ref)


def _combine_project_mlp(xt, outs, lses, expand, w_out, mlp_g, w1, w2, j, layer):
    t = xt.shape[0]
    assert ATTN_GROUPS[0][1] == 1
    dil = lambda w: [pl.BlockSpec((ROW_TILE // d, d * w), lambda i: (i, 0)) for _, d in ATTN_GROUPS]
    row = pl.BlockSpec((ROW_TILE, D_MODEL), lambda i: (i, 0))
    return pl.pallas_call(
        _combine_kernel,
        grid=(t // ROW_TILE,),
        in_specs=[row] + dil(ATTN_WIDTH) + dil(LANES)
                 + [_const_spec((LANES, ATTN_WIDTH)), _layer_spec((ATTN_WIDTH, D_MODEL), j)] + _mlp_specs(layer),
        out_specs=row,
        out_shape=jax.ShapeDtypeStruct(xt.shape, F32),
        scratch_shapes=[pltpu.VMEM((N_GROUPS - 1, ATTN_WIDTH // LANES, ROW_TILE, LANES), F32),
                        pltpu.VMEM((N_GROUPS - 1, ROW_TILE, LANES), F32)],
        compiler_params=_params(1),
        name="attn_combine_mlp",
    )(xt, *outs, *lses, expand, w_out, mlp_g.reshape(1, D_MODEL), w1, w2)


def _rope_tables():
    pos = jnp.arange(SEQ, dtype=F32)
    inv_freq = ROPE_THETA ** (-jnp.arange(0, HEAD_DIM, 2, dtype=F32) / HEAD_DIM)
    ang = pos[:, None] * inv_freq[None, :]
    return jnp.cos(ang), jnp.sin(ang)


def _dilated_table(tab, dilation):
    tt = _qkv_step_tokens(dilation)
    nl = tt // dilation
    return tab.reshape(SEQ // tt, nl, dilation, HALF).transpose(0, 3, 2, 1).reshape(SEQ // tt, HALF, tt)


def _attention_layer(xt, w_all, q_gain, k_gain, w_out, cos, sin, mlp_g, w1, w2, j, layer):
    x3 = xt.reshape(BATCH, SEQ, D_MODEL)
    lane_bcast = lambda v: jnp.broadcast_to(v.reshape(HEAD_DIM, 1), (HEAD_DIM, LANES))
    qg = lane_bcast(q_gain * (HEAD_DIM ** -0.5 * LOG2E))
    kg = lane_bcast(k_gain)
    head = jnp.arange(LANES)[:, None]
    lane = jnp.arange(ATTN_WIDTH)[None, :]
    expand = (lane // HEAD_DIM == head).astype(BF16)
    outs, lses = [], []
    for gi, (_, dilation) in enumerate(ATTN_GROUPS):
        qt, k, vt = _qkv_project(x3, w_all, j, gi, qg, kg, _dilated_table(cos, dilation),
                                 _dilated_table(sin, dilation), dilation)
        o, l = _attention(qt, k, vt, dilation)
        outs.append(o)
        lses.append(l)
    return _combine_project_mlp(xt, outs, lses, expand, w_out, mlp_g, w1, w2, j, layer)


def kernel(x, mixer_norm, mlp_norm, conv_w_in, conv_b_in, conv_w_dw, conv_b_dw, conv_ln_g, conv_ln_b,
           conv_w_out, conv_b_out, attn_w_in, attn_q_norm, attn_k_norm, attn_w_out, mlp_w1, mlp_w2):
    assert x.shape == (BATCH, SEQ, D_MODEL) and x.dtype == F32
    cos, sin = _rope_tables()
    w1_all, w2_all = mlp_w1.astype(BF16), mlp_w2.astype(BF16)
    conv_in_all, conv_out_all = conv_w_in.astype(BF16), conv_w_out.astype(BF16)
    attn_out_all = attn_w_out.astype(BF16)
    attn_gain = mixer_norm[1::2]
    attn_w_all = (attn_w_in * attn_gain[:, :, None]).astype(BF16)
    xt = x.reshape(BATCH * SEQ, D_MODEL)
    for i in range(DEPTH):
        j = i // 2
        if i % 2 == 0:
            x3 = _conv_layer(xt.reshape(BATCH, SEQ, D_MODEL), mixer_norm[i], conv_in_all, conv_b_in[j],
                             conv_w_dw[j], conv_b_dw[j], conv_ln_g[j], conv_ln_b[j], conv_out_all, conv_b_out[j],
                             mlp_norm[i], w1_all, w2_all, j, i)
            xt = x3.reshape(BATCH * SEQ, D_MODEL)
        else:
            xt = _attention_layer(xt, attn_w_all, attn_q_norm[j], attn_k_norm[j], attn_out_all, cos, sin,
                                  mlp_norm[i], w1_all, w2_all, j, i)
    return xt.reshape(BATCH, SEQ, D_MODEL)
```

```python
import functools

import jax
import jax.numpy as jnp
from jax import lax
from jax.experimental import pallas as pl
from jax.experimental.pallas import tpu as pltpu

D_MODEL = 1024
BATCH = 8
SEQ = 4096
DEPTH = 4
CONV_KERNEL = 31
ATTN_GROUPS = ((128, 1), (512, 4), (2048, 16))
N_GROUPS = len(ATTN_GROUPS)
HEADS = 8
HEAD_DIM = 64
HALF = HEAD_DIM // 2
ATTN_WIDTH = HEADS * HEAD_DIM
D_FF = 4 * D_MODEL
ROPE_THETA = 10000.0
EPS = 1e-6

LANES = 128
BF16_ROWS = 16
WIN = 128
V7X_VMEM_LIMIT = 56 * 1024 * 1024

ROW_TILE = 512
CONV_HALO = 32
CONV_CHUNK = 64
CONV_PAIR = CONV_CHUNK // 2
CONV_GROUP = 4
N_SLABS = D_MODEL // LANES
CONV_SLABS = N_SLABS
QKV_CHUNK = 512
ATTN_STEP_TOKENS = 2048
FF_CHUNK = 1024

F32 = jnp.float32
BF16 = jnp.bfloat16
NEG = -1e30
LN2 = 0.6931471805599453
LOG2E = 1.4426950408889634


def _params(n_axes):
    return pltpu.CompilerParams(
        dimension_semantics=("arbitrary",) * n_axes,
        vmem_limit_bytes=V7X_VMEM_LIMIT,
    )


def _const_spec(shape):
    nd = len(shape)
    return pl.BlockSpec(shape, lambda *_: (0,) * nd, pipeline_mode=pl.Buffered(1))


def _layer_spec(shape, layer):
    nd = len(shape)
    return pl.BlockSpec((None,) + tuple(shape), lambda *_: (layer,) + (0,) * nd, pipeline_mode=pl.Buffered(1))


def _rms_rows(x, g):
    return x * lax.rsqrt(jnp.mean(x * x, axis=-1, keepdims=True) + EPS) * g


def _sigmoid(x):
    return 1.0 / (1.0 + jnp.exp(-x))


def _mlp_rows(x, g_ref, w1_ref, w2_ref):
    h = _rms_rows(x, g_ref[...]).astype(BF16)
    acc = x
    for c in range(D_FF // FF_CHUNK):
        cs = slice(c * FF_CHUNK, (c + 1) * FF_CHUNK)
        z = jnp.dot(h, w1_ref[:, cs], preferred_element_type=F32)
        z = jnp.maximum(z, 0.0)
        acc = acc + jnp.dot((z * z).astype(BF16), w2_ref[cs, :], preferred_element_type=F32)
    return acc


def _mlp_specs(layer):
    return [_const_spec((1, D_MODEL)), _layer_spec((D_MODEL, D_FF), layer), _layer_spec((D_FF, D_MODEL), layer)]


def _conv_kernel(x_ref, g_ref, win_ref, bin_ref, wdw_ref, bdw_ref, lng_ref, lnb_ref, wout_ref, bout_ref,
                 mg_ref, w1_ref, w2_ref, o_ref, ubuf, pkbuf, cbuf):
    x = x_ref[...]
    h = _rms_rows(x, g_ref[...]).astype(BF16)
    a = jnp.dot(h, win_ref[...], preferred_element_type=F32) + bin_ref[...]
    u = (a[:, :D_MODEL] * _sigmoid(a[:, D_MODEL:])).astype(BF16).astype(F32)

    @pl.when(pl.program_id(1) == 0)
    def _():
        ubuf[:, 0:CONV_HALO, :] = jnp.zeros((CONV_SLABS, CONV_HALO, LANES), F32)

    n_words = CONV_HALO + ROW_TILE - CONV_PAIR
    for c in range(CONV_SLABS):
        ubuf[c, CONV_HALO:CONV_HALO + ROW_TILE, :] = u[:, c * LANES:(c + 1) * LANES]
        bits = pltpu.bitcast(ubuf[c], jnp.uint32)
        pkbuf[c] = lax.shift_right_logical(bits[:n_words], jnp.uint32(16)) | bits[CONV_PAIR:CONV_PAIR + n_words]
    first_tap = CONV_HALO - (CONV_KERNEL - 1)
    high_half = jnp.uint32(0xFFFF0000)

    def slab(c, carry):
        for base in range(0, ROW_TILE, CONV_CHUNK):
            acc_lo = jnp.broadcast_to(bdw_ref[c], (CONV_PAIR, LANES))
            acc_hi = acc_lo
            for k0 in range(0, CONV_KERNEL, CONV_GROUP):
                part = None
                for k in range(k0, min(k0 + CONV_GROUP, CONV_KERNEL)):
                    lo = base + first_tap + k
                    win = pltpu.bitcast(pkbuf[c, lo:lo + CONV_PAIR, :], BF16)
                    wk = jnp.concatenate([wdw_ref[c, k]] * (CONV_CHUNK // BF16_ROWS), axis=0)
                    part = win * wk if part is None else part + win * wk
                words = pltpu.bitcast(part, jnp.uint32)
                acc_lo = acc_lo + pltpu.bitcast(lax.shift_left(words, jnp.uint32(16)), F32)
                acc_hi = acc_hi + pltpu.bitcast(words & high_half, F32)
            cbuf[c, base:base + CONV_PAIR, :] = acc_lo
            cbuf[c, base + CONV_PAIR:base + CONV_CHUNK, :] = acc_hi
        return carry

    lax.fori_loop(0, CONV_SLABS, slab, 0)
    for c in range(CONV_SLABS):
        ubuf[c, 0:CONV_HALO, :] = ubuf[c, ROW_TILE:ROW_TILE + CONV_HALO, :]
    conv = jnp.concatenate([cbuf[c] for c in range(CONV_SLABS)], axis=1)
    mu = jnp.mean(conv, axis=-1, keepdims=True)
    xc = conv - mu
    y = xc * lax.rsqrt(jnp.mean(xc * xc, axis=-1, keepdims=True) + EPS) * lng_ref[...] + lnb_ref[...]
    y = (y * _sigmoid(y)).astype(BF16)
    x_mid = x + jnp.dot(y, wout_ref[...], preferred_element_type=F32) + bout_ref[...]
    o_ref[...] = _mlp_rows(x_mid, mg_ref, w1_ref, w2_ref)


def _conv_layer(x3, g, w_in, b_in, w_dw, b_dw, ln_g, ln_b, w_out, b_out, mlp_g, w1, w2, j, layer):
    row = pl.BlockSpec((None, ROW_TILE, D_MODEL), lambda b, j: (b, j, 0))
    vec = lambda v: v.reshape(1, -1)
    w_slabs = jnp.broadcast_to(w_dw.reshape(CONV_KERNEL, CONV_SLABS, 1, LANES).transpose(1, 0, 2, 3),
                               (CONV_SLABS, CONV_KERNEL, BF16_ROWS, LANES)).astype(BF16)
    b_slabs = b_dw.reshape(CONV_SLABS, 1, LANES)
    return pl.pallas_call(
        _conv_kernel,
        grid=(BATCH, SEQ // ROW_TILE),
        in_specs=[row, _const_spec((1, D_MODEL)), _layer_spec((D_MODEL, 2 * D_MODEL), j),
                  _const_spec((1, 2 * D_MODEL)), _const_spec((CONV_SLABS, CONV_KERNEL, BF16_ROWS, LANES)),
                  _const_spec((CONV_SLABS, 1, LANES)), _const_spec((1, D_MODEL)), _const_spec((1, D_MODEL)),
                  _layer_spec((D_MODEL, D_MODEL), j), _const_spec((1, D_MODEL))] + _mlp_specs(layer),
        out_specs=row,
        out_shape=jax.ShapeDtypeStruct(x3.shape, F32),
        scratch_shapes=[pltpu.VMEM((CONV_SLABS, ROW_TILE + CONV_HALO, LANES), F32),
                        pltpu.VMEM((CONV_SLABS, ROW_TILE + CONV_HALO - CONV_PAIR, LANES), jnp.uint32),
                        pltpu.VMEM((CONV_SLABS, ROW_TILE, LANES), F32)],
        compiler_params=_params(2),
        name="conv_layer",
    )(x3, vec(g), w_in, vec(b_in), w_slabs, b_slabs, vec(ln_g), vec(ln_b), w_out, vec(b_out), vec(mlp_g), w1, w2)


def _group_tiles(dilation):
    return dilation, ATTN_STEP_TOKENS // dilation


def _qkv_step_tokens(dilation):
    return max(4 * QKV_CHUNK, WIN * dilation)


def _qkv_kernel(*refs, dilation, tt):
    x_refs = refs[:N_SLABS]
    w_refs = refs[N_SLABS:N_SLABS + 3]
    qg_ref, kg_ref, cos_ref, sin_ref, qt_ref, k_ref, vt_ref, rs_scr = refs[N_SLABS + 3:N_SLABS + 11]
    nl = tt // dilation
    ssq = x_refs[0][...] * x_refs[0][...]
    for c in range(1, N_SLABS):
        ssq = ssq + x_refs[c][...] * x_refs[c][...]
    rs = lax.rsqrt(jnp.sum(ssq, axis=-1, keepdims=True) * (1.0 / D_MODEL) + EPS)
    rs_scr[...] = jnp.broadcast_to(rs, (tt, LANES))

    if dilation == 16:
        a_scr, rsa_scr = refs[N_SLABS + 11:]
        quarter = tt // 4
        for q in range(4):
            dst = slice(q * quarter, (q + 1) * quarter)
            for c in range(N_SLABS):
                a_scr[c, dst, :] = x_refs[c][pl.ds(q, quarter, stride=4), :]
            rsa_scr[dst, :] = rs_scr[pl.ds(q, quarter, stride=4), :]

    def residue_rows(r, lo, n):
        if dilation == 1:
            idx = pl.ds(lo, n)
            return [x_refs[c][idx, :] for c in range(N_SLABS)], rs_scr[idx, :]
        if dilation == 4:
            idx = pl.ds(r + lo * 4, n, stride=4)
            return [x_refs[c][idx, :] for c in range(N_SLABS)], rs_scr[idx, :]
        idx = pl.ds((r % 4) * (tt // 4) + r // 4 + lo * 4, n, stride=4)
        return [a_scr[c, idx, :] for c in range(N_SLABS)], rsa_scr[idx, :]

    widen = lambda t: jnp.concatenate([t] * (QKV_CHUNK // LANES), axis=1)
    gains = (widen(qg_ref[...]), widen(kg_ref[...]))
    for ck in range(tt // QKV_CHUNK):
        if nl >= QKV_CHUNK:
            per = nl // QKV_CHUNK
            pieces = [(ck // per, (ck % per) * QKV_CHUNK, QKV_CHUNK)]
        else:
            pieces = [(ck * (QKV_CHUNK // nl) + i, 0, nl) for i in range(QKV_CHUNK // nl)]
        hs = []
        for r, lo, n in pieces:
            slabs, scale = residue_rows(r, lo, n)
            hs.append(jnp.concatenate([slabs[c] * scale for c in range(N_SLABS)], axis=1).astype(BF16))
        h = jnp.concatenate(hs, axis=0) if len(hs) > 1 else hs[0]
        ys = [lax.dot_general(w_ref[...], h, (((0,), (1,)), ((), ())), preferred_element_type=F32)
              for w_ref in w_refs]
        cols = slice(ck * QKV_CHUNK, (ck + 1) * QKV_CHUNK)
        cos = cos_ref[:, cols]
        sin = sin_ref[:, cols]
        rot = []
        for s in range(2):
            g1, g2 = gains[s][:HALF], gains[s][HALF:]
            c1, s2, c2, s1 = g1 * cos, g2 * sin, g2 * cos, g1 * sin
            parts = []
            for hh in range(HEADS):
                t = ys[s][hh * HEAD_DIM:(hh + 1) * HEAD_DIM, :]
                inv = lax.rsqrt(jnp.mean(t * t, axis=0, keepdims=True) + EPS)
                t1, t2 = t[:HALF], t[HALF:]
                parts += [(t1 * c1 - t2 * s2) * inv, (t2 * c2 + t1 * s1) * inv]
            rot.append(jnp.concatenate(parts, axis=0))
        q_t = rot[0].astype(BF16)
        k_rows = rot[1].T.astype(BF16)
        v_t = ys[2].astype(BF16)
        off = 0
        for r, lo, n in pieces:
            qt_ref[r, :, lo:lo + n] = q_t[:, off:off + n]
            k_ref[r, lo:lo + n, :] = k_rows[off:off + n, :]
            vt_ref[r, :, lo:lo + n] = v_t[:, off:off + n]
            off += n


def _qkv_project(x3, w_all, j, group, qg, kg, cos_t, sin_t, dilation):
    tt = _qkv_step_tokens(dilation)
    n_sub = SEQ // dilation
    nl = tt // dilation
    slab = lambda c: pl.BlockSpec((None, tt, LANES), lambda b, j: (b, j, c))
    t_spec = pl.BlockSpec((None, dilation, ATTN_WIDTH, nl), lambda b, j: (b, 0, 0, j))
    r_spec = pl.BlockSpec((None, dilation, nl, ATTN_WIDTH), lambda b, j: (b, 0, j, 0))
    tab = pl.BlockSpec((None, HALF, tt), lambda b, j: (j, 0, 0))
    scratch = [pltpu.VMEM((tt, LANES), F32)]
    if dilation == 16:
        scratch += [pltpu.VMEM((N_SLABS, tt, LANES), F32), pltpu.VMEM((tt, LANES), F32)]
    return pl.pallas_call(
        functools.partial(_qkv_kernel, dilation=dilation, tt=tt),
        grid=(BATCH, SEQ // tt),
        in_specs=[slab(c) for c in range(N_SLABS)]
                 + [pl.BlockSpec((None, D_MODEL, ATTN_WIDTH), lambda b, i, s=s: (j, 0, s * N_GROUPS + group),
                                 pipeline_mode=pl.Buffered(1)) for s in range(3)]
                 + [_const_spec((HEAD_DIM, LANES)), _const_spec((HEAD_DIM, LANES)), tab, tab],
        out_specs=[t_spec, r_spec, t_spec],
        out_shape=[jax.ShapeDtypeStruct((BATCH, dilation, ATTN_WIDTH, n_sub), BF16),
                   jax.ShapeDtypeStruct((BATCH, dilation, n_sub, ATTN_WIDTH), BF16),
                   jax.ShapeDtypeStruct((BATCH, dilation, ATTN_WIDTH, n_sub), BF16)],
        scratch_shapes=scratch,
        compiler_params=_params(2),
        name=f"qkv_d{dilation}",
    )(*([x3] * N_SLABS), w_all, w_all, w_all, qg, kg, cos_t, sin_t)


def _attn_kernel(qt_ref, kc_ref, kp_ref, vc_ref, vp_ref, o0_ref, o1_ref, o2_ref, o3_ref, lse_ref, s_scr, p_scr,
                 *, rb_count, qb):
    o_refs = (o0_ref, o1_ref, o2_ref, o3_ref)
    kj = lax.broadcasted_iota(jnp.int32, (WIN, WIN), 0)
    qi = lax.broadcasted_iota(jnp.int32, (WIN, WIN), 1)
    eye = jnp.where(kj == qi, 1.0, 0.0).astype(BF16)
    two = lambda m: jnp.concatenate([m, m], axis=1).astype(BF16)
    mask_prev = two(jnp.where(kj >= qi, 0.0, NEG))
    mask_cur = two(jnp.where(kj <= qi, 0.0, NEG))
    mask_none = two(jnp.full((WIN, WIN), NEG, F32))
    mask_head = jnp.where(pl.program_id(2) > 0, mask_prev, mask_none)
    zeros = jnp.zeros((HEAD_DIM, WIN), BF16)
    ones = jnp.ones((BF16_ROWS, 2 * WIN), BF16)
    pad = jnp.zeros((LANES - HEADS, WIN), F32)
    units = [(rb, i, p) for rb in range(rb_count) for i in range(qb // WIN) for p in range(HEADS // 2)]

    def slices(i, p):
        return (slice(i * WIN, (i + 1) * WIN), slice((i - 1) * WIN, i * WIN),
                slice(p * 2 * HEAD_DIM, (p + 1) * 2 * HEAD_DIM))

    for n, (rb, i, p) in enumerate(units):
        cur, prev, cols = slices(i, p)
        k_prev = kp_ref[rb, :, cols] if i == 0 else kc_ref[rb, prev, cols]
        q2 = qt_ref[rb, cols, cur]
        qbd = jnp.concatenate(
            [jnp.concatenate([q2[:HEAD_DIM], zeros], axis=1),
             jnp.concatenate([zeros, q2[HEAD_DIM:]], axis=1)], axis=0)
        m_prev = mask_head if i == 0 else mask_prev
        s_scr[n, 0:WIN, :] = jnp.dot(jnp.concatenate([k_prev, eye], axis=1),
                                     jnp.concatenate([qbd, m_prev], axis=0), preferred_element_type=F32)
        s_scr[n, WIN:2 * WIN, :] = jnp.dot(jnp.concatenate([kc_ref[rb, cur, cols], eye], axis=1),
                                           jnp.concatenate([qbd, mask_cur], axis=0), preferred_element_type=F32)

    stats = []
    for n in range(len(units)):
        s = s_scr[n]
        m = jnp.max(s, axis=0, keepdims=True)
        e = jnp.exp2(s - m)
        p_scr[n] = e.astype(BF16)
        stats.append(m)

    outs, lses = {}, {}
    for n, (rb, i, p) in enumerate(units):
        cur, prev, cols = slices(i, p)
        v_prev = vp_ref[rb, cols, :] if i == 0 else vc_ref[rb, cols, prev]
        vv = jnp.concatenate([v_prev, vc_ref[rb, cols, cur]], axis=1)
        o2 = jnp.dot(jnp.concatenate([vv, ones], axis=0), p_scr[n], preferred_element_type=F32)
        m, den = stats[n], o2[2 * HEAD_DIM:2 * HEAD_DIM + 1, :]
        inv = 1.0 / den
        outs.setdefault((rb, i), []).extend(
            [o2[:HEAD_DIM, :WIN] * inv[:, :WIN], o2[HEAD_DIM:2 * HEAD_DIM, WIN:] * inv[:, WIN:]])
        lse = m * LN2 + jnp.log(den)
        lses.setdefault((rb, i), []).extend([lse[:, :WIN], lse[:, WIN:]])
    for (rb, i), pieces in outs.items():
        rows = pl.ds(rb + i * WIN * rb_count, WIN, stride=rb_count) if rb_count > 1 else pl.ds(i * WIN, WIN)
        o_t = jnp.concatenate(pieces, axis=0).T
        for c, o_ref in enumerate(o_refs):
            o_ref[rows, :] = o_t[:, c * LANES:(c + 1) * LANES]
        lse_ref[rows, :] = jnp.concatenate(lses[(rb, i)] + [pad], axis=0).T


def _attention(qt, k, vt, dilation):
    rb, qb = _group_tiles(dilation)
    n_sub = SEQ // dilation
    per = qb // WIN
    prev_blk = lambda j: jnp.maximum(j * per - 1, 0)
    tok = pl.BlockSpec((None, ATTN_STEP_TOKENS, LANES), lambda b, r, j: (b, j, 0))
    tok_shape = jax.ShapeDtypeStruct((BATCH, SEQ, LANES), F32)
    *outs, lse = pl.pallas_call(
        functools.partial(_attn_kernel, rb_count=rb, qb=qb),
        grid=(BATCH, dilation // rb, n_sub // qb),
        in_specs=[pl.BlockSpec((None, rb, ATTN_WIDTH, qb), lambda b, r, j: (b, r, 0, j)),
                  pl.BlockSpec((None, rb, qb, ATTN_WIDTH), lambda b, r, j: (b, r, j, 0)),
                  pl.BlockSpec((None, rb, WIN, ATTN_WIDTH), lambda b, r, j: (b, r, prev_blk(j), 0)),
                  pl.BlockSpec((None, rb, ATTN_WIDTH, qb), lambda b, r, j: (b, r, 0, j)),
                  pl.BlockSpec((None, rb, ATTN_WIDTH, WIN), lambda b, r, j: (b, r, 0, prev_blk(j)))],
        out_specs=[tok] * (ATTN_WIDTH // LANES + 1),
        out_shape=[tok_shape] * (ATTN_WIDTH // LANES + 1),
        scratch_shapes=[pltpu.VMEM((rb * per * (HEADS // 2), 2 * WIN, 2 * WIN), F32),
                        pltpu.VMEM((rb * per * (HEADS // 2), 2 * WIN, 2 * WIN), BF16)],
        compiler_params=_params(3),
        name=f"attn_d{dilation}",
    )(qt, k, k, vt, vt)
    flat = lambda a: a.reshape(BATCH * SEQ, LANES)
    return [flat(o) for o in outs], flat(lse)


def _combine_kernel(*refs):
    n_slab = ATTN_WIDTH // LANES
    x_ref = refs[0]
    o_refs = refs[1:1 + N_GROUPS * n_slab]
    l_refs = refs[1 + N_GROUPS * n_slab:1 + N_GROUPS * (n_slab + 1)]
    ex_ref, w_ref, mg_ref, w1_ref, w2_ref, out_ref = refs[1 + N_GROUPS * (n_slab + 1):]
    os = [jnp.concatenate([o_refs[g * n_slab + c][...] for c in range(n_slab)], axis=1) for g in range(N_GROUPS)]
    ls = [l_ref[...] for l_ref in l_refs]
    m = jnp.maximum(jnp.maximum(ls[0], ls[1]), ls[2])
    es = [jnp.exp(l - m) for l in ls]
    inv = 1.0 / (es[0] + es[1] + es[2])
    mixed = jnp.zeros((ROW_TILE, ATTN_WIDTH), F32)
    for e, o in zip(es, os):
        wide = jnp.dot((e * inv).astype(BF16), ex_ref[...], preferred_element_type=F32)
        mixed = mixed + wide * o
    x_mid = x_ref[...] + jnp.dot(mixed.astype(BF16), w_ref[...], preferred_element_type=F32)
    out_ref[...] = _mlp_rows(x_mid, mg_ref, w1_ref, w2_ref)


def _combine_project_mlp(xt, outs, lses, expand, w_out, mlp_g, w1, w2, j, layer):
    t = xt.shape[0]
    slab = pl.BlockSpec((ROW_TILE, LANES), lambda i: (i, 0))
    row = pl.BlockSpec((ROW_TILE, D_MODEL), lambda i: (i, 0))
    return pl.pallas_call(
        _combine_kernel,
        grid=(t // ROW_TILE,),
        in_specs=[row] + [slab] * (N_GROUPS * (ATTN_WIDTH // LANES + 1))
                 + [_const_spec((LANES, ATTN_WIDTH)), _layer_spec((ATTN_WIDTH, D_MODEL), j)] + _mlp_specs(layer),
        out_specs=row,
        out_shape=jax.ShapeDtypeStruct(xt.shape, F32),
        compiler_params=_params(1),
        name="attn_combine_mlp",
    )(xt, *[o for group in outs for o in group], *lses, expand, w_out, mlp_g.reshape(1, D_MODEL), w1, w2)


def _rope_tables():
    pos = jnp.arange(SEQ, dtype=F32)
    inv_freq = ROPE_THETA ** (-jnp.arange(0, HEAD_DIM, 2, dtype=F32) / HEAD_DIM)
    ang = pos[:, None] * inv_freq[None, :]
    return jnp.cos(ang), jnp.sin(ang)


def _dilated_table(tab, dilation):
    tt = _qkv_step_tokens(dilation)
    nl = tt // dilation
    return tab.reshape(SEQ // tt, nl, dilation, HALF).transpose(0, 3, 2, 1).reshape(SEQ // tt, HALF, tt)


def _attention_layer(xt, w_all, q_gain, k_gain, w_out, cos, sin, mlp_g, w1, w2, j, layer):
    x3 = xt.reshape(BATCH, SEQ, D_MODEL)
    lane_bcast = lambda v: jnp.broadcast_to(v.reshape(HEAD_DIM, 1), (HEAD_DIM, LANES))
    qg = lane_bcast(q_gain * (HEAD_DIM ** -0.5 * LOG2E))
    kg = lane_bcast(k_gain)
    head = jnp.arange(LANES)[:, None]
    lane = jnp.arange(ATTN_WIDTH)[None, :]
    expand = (lane // HEAD_DIM == head).astype(BF16)
    outs, lses = [], []
    for gi, (_, dilation) in enumerate(ATTN_GROUPS):
        qt, k, vt = _qkv_project(x3, w_all, j, gi, qg, kg, _dilated_table(cos, dilation),
                                 _dilated_table(sin, dilation), dilation)
        o, l = _attention(qt, k, vt, dilation)
        outs.append(o)
        lses.append(l)
    return _combine_project_mlp(xt, outs, lses, expand, w_out, mlp_g, w1, w2, j, layer)


def kernel(x, mixer_norm, mlp_norm, conv_w_in, conv_b_in, conv_w_dw, conv_b_dw, conv_ln_g, conv_ln_b,
           conv_w_out, conv_b_out, attn_w_in, attn_q_norm, attn_k_norm, attn_w_out, mlp_w1, mlp_w2):
    assert x.shape == (BATCH, SEQ, D_MODEL) and x.dtype == F32
    cos, sin = _rope_tables()
    w1_all, w2_all = mlp_w1.astype(BF16), mlp_w2.astype(BF16)
    conv_in_all, conv_out_all = conv_w_in.astype(BF16), conv_w_out.astype(BF16)
    attn_out_all = attn_w_out.astype(BF16)
    attn_gain = mixer_norm[1::2]
    attn_w_all = (attn_w_in * attn_gain[:, :, None]).astype(BF16)
    xt = x.reshape(BATCH * SEQ, D_MODEL)
    for i in range(DEPTH):
        j = i // 2
        if i % 2 == 0:
            x3 = _conv_layer(xt.reshape(BATCH, SEQ, D_MODEL), mixer_norm[i], conv_in_all, conv_b_in[j],
                             conv_w_dw[j], conv_b_dw[j], conv_ln_g[j], conv_ln_b[j], conv_out_all, conv_b_out[j],
                             mlp_norm[i], w1_all, w2_all, j, i)
            xt = x3.reshape(BATCH * SEQ, D_MODEL)
        else:
            xt = _attention_layer(xt, attn_w_all, attn_q_norm[j], attn_k_norm[j], attn_out_all, cos, sin,
                                  mlp_norm[i], w1_all, w2_all, j, i)
    return xt.reshape(BATCH, SEQ, D_MODEL)
```

```python
import functools

import jax
import jax.numpy as jnp
from jax import lax
from jax.experimental import pallas as pl
from jax.experimental.pallas import tpu as pltpu

D_MODEL = 1024
BATCH = 8
SEQ = 4096
DEPTH = 4
CONV_KERNEL = 31
ATTN_GROUPS = ((128, 1), (512, 4), (2048, 16))
N_GROUPS = len(ATTN_GROUPS)
HEADS = 8
HEAD_DIM = 64
HALF = HEAD_DIM // 2
ATTN_WIDTH = HEADS * HEAD_DIM
D_FF = 4 * D_MODEL
ROPE_THETA = 10000.0
EPS = 1e-6

LANES = 128
BF16_ROWS = 16
WIN = 128
V7X_VMEM_LIMIT = 56 * 1024 * 1024

ROW_TILE = 512
CONV_HALO = 32
CONV_CHUNK = 64
CONV_PAIR = CONV_CHUNK // 2
CONV_GROUP = 4
N_SLABS = D_MODEL // LANES
CONV_SLABS = N_SLABS
QKV_CHUNK = 512
ATTN_STEP_TOKENS = 2048
FF_CHUNK = 1024

F32 = jnp.float32
BF16 = jnp.bfloat16
NEG = -1e30
LN2 = 0.6931471805599453
LOG2E = 1.4426950408889634


def _params(n_axes):
    return pltpu.CompilerParams(
        dimension_semantics=("arbitrary",) * n_axes,
        vmem_limit_bytes=V7X_VMEM_LIMIT,
    )


def _const_spec(shape):
    nd = len(shape)
    return pl.BlockSpec(shape, lambda *_: (0,) * nd, pipeline_mode=pl.Buffered(1))


def _layer_spec(shape, layer):
    nd = len(shape)
    return pl.BlockSpec((None,) + tuple(shape), lambda *_: (layer,) + (0,) * nd, pipeline_mode=pl.Buffered(1))


def _rms_rows(x, g):
    return x * lax.rsqrt(jnp.mean(x * x, axis=-1, keepdims=True) + EPS) * g


def _sigmoid(x):
    return 0.5 * jnp.tanh(0.5 * x) + 0.5


def _mlp_rows(x, g_ref, w1_ref, w2_ref):
    h = _rms_rows(x, g_ref[...]).astype(BF16)
    acc = x
    for c in range(D_FF // FF_CHUNK):
        cs = slice(c * FF_CHUNK, (c + 1) * FF_CHUNK)
        z = jnp.dot(h, w1_ref[:, cs], preferred_element_type=F32)
        z = jnp.maximum(z, 0.0)
        acc = acc + jnp.dot((z * z).astype(BF16), w2_ref[cs, :], preferred_element_type=F32)
    return acc


def _mlp_specs(layer):
    return [_const_spec((1, D_MODEL)), _layer_spec((D_MODEL, D_FF), layer), _layer_spec((D_FF, D_MODEL), layer)]


def _conv_kernel(x_ref, g_ref, win_ref, bin_ref, wdw_ref, bdw_ref, lng_ref, lnb_ref, wout_ref, bout_ref,
                 mg_ref, w1_ref, w2_ref, o_ref, ubuf, pkbuf, cbuf):
    x = x_ref[...]
    h = _rms_rows(x, g_ref[...]).astype(BF16)
    a = jnp.dot(h, win_ref[...], preferred_element_type=F32) + bin_ref[...]
    u = (a[:, :D_MODEL] * _sigmoid(a[:, D_MODEL:])).astype(BF16).astype(F32)

    @pl.when(pl.program_id(1) == 0)
    def _():
        ubuf[:, 0:CONV_HALO, :] = jnp.zeros((CONV_SLABS, CONV_HALO, LANES), F32)

    n_words = CONV_HALO + ROW_TILE - CONV_PAIR
    for c in range(CONV_SLABS):
        ubuf[c, CONV_HALO:CONV_HALO + ROW_TILE, :] = u[:, c * LANES:(c + 1) * LANES]
        bits = pltpu.bitcast(ubuf[c], jnp.uint32)
        pkbuf[c] = lax.shift_right_logical(bits[:n_words], jnp.uint32(16)) | bits[CONV_PAIR:CONV_PAIR + n_words]
    first_tap = CONV_HALO - (CONV_KERNEL - 1)
    high_half = jnp.uint32(0xFFFF0000)

    def slab(c, carry):
        for base in range(0, ROW_TILE, CONV_CHUNK):
            acc_lo = jnp.broadcast_to(bdw_ref[c], (CONV_PAIR, LANES))
            acc_hi = acc_lo
            for k0 in range(0, CONV_KERNEL, CONV_GROUP):
                part = None
                for k in range(k0, min(k0 + CONV_GROUP, CONV_KERNEL)):
                    lo = base + first_tap + k
                    win = pltpu.bitcast(pkbuf[c, lo:lo + CONV_PAIR, :], BF16)
                    wk = jnp.concatenate([wdw_ref[c, k]] * (CONV_CHUNK // BF16_ROWS), axis=0)
                    part = win * wk if part is None else part + win * wk
                words = pltpu.bitcast(part, jnp.uint32)
                acc_lo = acc_lo + pltpu.bitcast(lax.shift_left(words, jnp.uint32(16)), F32)
                acc_hi = acc_hi + pltpu.bitcast(words & high_half, F32)
            cbuf[c, base:base + CONV_PAIR, :] = acc_lo
            cbuf[c, base + CONV_PAIR:base + CONV_CHUNK, :] = acc_hi
        return carry

    lax.fori_loop(0, CONV_SLABS, slab, 0)
    for c in range(CONV_SLABS):
        ubuf[c, 0:CONV_HALO, :] = ubuf[c, ROW_TILE:ROW_TILE + CONV_HALO, :]
    conv = jnp.concatenate([cbuf[c] for c in range(CONV_SLABS)], axis=1)
    mu = jnp.mean(conv, axis=-1, keepdims=True)
    xc = conv - mu
    y = xc * lax.rsqrt(jnp.mean(xc * xc, axis=-1, keepdims=True) + EPS) * lng_ref[...] + lnb_ref[...]
    y = (y * _sigmoid(y)).astype(BF16)
    x_mid = x + jnp.dot(y, wout_ref[...], preferred_element_type=F32) + bout_ref[...]
    o_ref[...] = _mlp_rows(x_mid, mg_ref, w1_ref, w2_ref)


def _conv_layer(x3, g, w_in, b_in, w_dw, b_dw, ln_g, ln_b, w_out, b_out, mlp_g, w1, w2, j, layer):
    row = pl.BlockSpec((None, ROW_TILE, D_MODEL), lambda b, j: (b, j, 0))
    vec = lambda v: v.reshape(1, -1)
    w_slabs = jnp.broadcast_to(w_dw.reshape(CONV_KERNEL, CONV_SLABS, 1, LANES).transpose(1, 0, 2, 3),
                               (CONV_SLABS, CONV_KERNEL, BF16_ROWS, LANES)).astype(BF16)
    b_slabs = b_dw.reshape(CONV_SLABS, 1, LANES)
    return pl.pallas_call(
        _conv_kernel,
        grid=(BATCH, SEQ // ROW_TILE),
        in_specs=[row, _const_spec((1, D_MODEL)), _layer_spec((D_MODEL, 2 * D_MODEL), j),
                  _const_spec((1, 2 * D_MODEL)), _const_spec((CONV_SLABS, CONV_KERNEL, BF16_ROWS, LANES)),
                  _const_spec((CONV_SLABS, 1, LANES)), _const_spec((1, D_MODEL)), _const_spec((1, D_MODEL)),
                  _layer_spec((D_MODEL, D_MODEL), j), _const_spec((1, D_MODEL))] + _mlp_specs(layer),
        out_specs=row,
        out_shape=jax.ShapeDtypeStruct(x3.shape, F32),
        scratch_shapes=[pltpu.VMEM((CONV_SLABS, ROW_TILE + CONV_HALO, LANES), F32),
                        pltpu.VMEM((CONV_SLABS, ROW_TILE + CONV_HALO - CONV_PAIR, LANES), jnp.uint32),
                        pltpu.VMEM((CONV_SLABS, ROW_TILE, LANES), F32)],
        compiler_params=_params(2),
        name="conv_layer",
    )(x3, vec(g), w_in, vec(b_in), w_slabs, b_slabs, vec(ln_g), vec(ln_b), w_out, vec(b_out), vec(mlp_g), w1, w2)


def _group_tiles(dilation):
    n_sub = SEQ // dilation
    tl = min(n_sub, ATTN_STEP_TOKENS)
    return ATTN_STEP_TOKENS // tl, tl


def _qkv_step_tokens(dilation):
    return max(4 * QKV_CHUNK, WIN * dilation)


def _qkv_kernel(*refs, dilation, tt):
    x_refs = refs[:N_SLABS]
    w_refs = refs[N_SLABS:N_SLABS + 3]
    qg_ref, kg_ref, cos_ref, sin_ref, qt_ref, k_ref, vt_ref, rs_scr = refs[N_SLABS + 3:N_SLABS + 11]
    nl = tt // dilation
    ssq = x_refs[0][...] * x_refs[0][...]
    for c in range(1, N_SLABS):
        ssq = ssq + x_refs[c][...] * x_refs[c][...]
    rs = lax.rsqrt(jnp.sum(ssq, axis=-1, keepdims=True) * (1.0 / D_MODEL) + EPS)
    rs_scr[...] = jnp.broadcast_to(rs, (tt, LANES))

    if dilation == 16:
        a_scr, rsa_scr = refs[N_SLABS + 11:]
        quarter = tt // 4
        for q in range(4):
            dst = slice(q * quarter, (q + 1) * quarter)
            for c in range(N_SLABS):
                a_scr[c, dst, :] = x_refs[c][pl.ds(q, quarter, stride=4), :]
            rsa_scr[dst, :] = rs_scr[pl.ds(q, quarter, stride=4), :]

    def residue_rows(r, lo, n):
        if dilation == 1:
            idx = pl.ds(lo, n)
            return [x_refs[c][idx, :] for c in range(N_SLABS)], rs_scr[idx, :]
        if dilation == 4:
            idx = pl.ds(r + lo * 4, n, stride=4)
            return [x_refs[c][idx, :] for c in range(N_SLABS)], rs_scr[idx, :]
        idx = pl.ds((r % 4) * (tt // 4) + r // 4 + lo * 4, n, stride=4)
        return [a_scr[c, idx, :] for c in range(N_SLABS)], rsa_scr[idx, :]

    widen = lambda t: jnp.concatenate([t] * (QKV_CHUNK // LANES), axis=1)
    gains = (widen(qg_ref[...]), widen(kg_ref[...]))
    for ck in range(tt // QKV_CHUNK):
        if nl >= QKV_CHUNK:
            per = nl // QKV_CHUNK
            pieces = [(ck // per, (ck % per) * QKV_CHUNK, QKV_CHUNK)]
        else:
            pieces = [(ck * (QKV_CHUNK // nl) + i, 0, nl) for i in range(QKV_CHUNK // nl)]
        hs = []
        for r, lo, n in pieces:
            slabs, scale = residue_rows(r, lo, n)
            hs.append(jnp.concatenate([slabs[c] * scale for c in range(N_SLABS)], axis=1).astype(BF16))
        h = jnp.concatenate(hs, axis=0) if len(hs) > 1 else hs[0]
        ys = [lax.dot_general(w_ref[...], h, (((0,), (1,)), ((), ())), preferred_element_type=F32)
              for w_ref in w_refs]
        cols = slice(ck * QKV_CHUNK, (ck + 1) * QKV_CHUNK)
        cos = cos_ref[:, cols]
        sin = sin_ref[:, cols]
        rot = []
        for s in range(2):
            g1, g2 = gains[s][:HALF], gains[s][HALF:]
            c1, s2, c2, s1 = g1 * cos, g2 * sin, g2 * cos, g1 * sin
            parts = []
            for hh in range(HEADS):
                t = ys[s][hh * HEAD_DIM:(hh + 1) * HEAD_DIM, :]
                inv = lax.rsqrt(jnp.mean(t * t, axis=0, keepdims=True) + EPS)
                t1, t2 = t[:HALF], t[HALF:]
                parts += [(t1 * c1 - t2 * s2) * inv, (t2 * c2 + t1 * s1) * inv]
            rot.append(jnp.concatenate(parts, axis=0))
        q_t = rot[0].astype(BF16)
        k_rows = rot[1].T.astype(BF16)
        v_t = ys[2].astype(BF16)
        off = 0
        for r, lo, n in pieces:
            qt_ref[r, :, lo:lo + n] = q_t[:, off:off + n]
            k_ref[r, lo:lo + n, :] = k_rows[off:off + n, :]
            vt_ref[r, :, lo:lo + n] = v_t[:, off:off + n]
            off += n


def _qkv_project(x3, w_all, j, group, qg, kg, cos_t, sin_t, dilation):
    tt = _qkv_step_tokens(dilation)
    n_sub = SEQ // dilation
    nl = tt // dilation
    slab = lambda c: pl.BlockSpec((None, tt, LANES), lambda b, j: (b, j, c))
    t_spec = pl.BlockSpec((None, dilation, ATTN_WIDTH, nl), lambda b, j: (b, 0, 0, j))
    r_spec = pl.BlockSpec((None, dilation, nl, ATTN_WIDTH), lambda b, j: (b, 0, j, 0))
    tab = pl.BlockSpec((None, HALF, tt), lambda b, j: (j, 0, 0))
    scratch = [pltpu.VMEM((tt, LANES), F32)]
    if dilation == 16:
        scratch += [pltpu.VMEM((N_SLABS, tt, LANES), F32), pltpu.VMEM((tt, LANES), F32)]
    return pl.pallas_call(
        functools.partial(_qkv_kernel, dilation=dilation, tt=tt),
        grid=(BATCH, SEQ // tt),
        in_specs=[slab(c) for c in range(N_SLABS)]
                 + [pl.BlockSpec((None, D_MODEL, ATTN_WIDTH), lambda b, i, s=s: (j, 0, s * N_GROUPS + group),
                                 pipeline_mode=pl.Buffered(1)) for s in range(3)]
                 + [_const_spec((HEAD_DIM, LANES)), _const_spec((HEAD_DIM, LANES)), tab, tab],
        out_specs=[t_spec, r_spec, t_spec],
        out_shape=[jax.ShapeDtypeStruct((BATCH, dilation, ATTN_WIDTH, n_sub), BF16),
                   jax.ShapeDtypeStruct((BATCH, dilation, n_sub, ATTN_WIDTH), BF16),
                   jax.ShapeDtypeStruct((BATCH, dilation, ATTN_WIDTH, n_sub), BF16)],
        scratch_shapes=scratch,
        compiler_params=_params(2),
        name=f"qkv_d{dilation}",
    )(*([x3] * N_SLABS), w_all, w_all, w_all, qg, kg, cos_t, sin_t)


def _attn_kernel(qt_ref, kc_ref, kp_ref, vc_ref, vp_ref, o_ref, lse_ref, s_scr, p_scr, *, rb_count, qb):
    kj = lax.broadcasted_iota(jnp.int32, (WIN, WIN), 0)
    qi = lax.broadcasted_iota(jnp.int32, (WIN, WIN), 1)
    eye = jnp.where(kj == qi, 1.0, 0.0).astype(BF16)
    two = lambda m: jnp.concatenate([m, m], axis=1).astype(BF16)
    mask_prev = two(jnp.where(kj >= qi, 0.0, NEG))
    mask_cur = two(jnp.where(kj <= qi, 0.0, NEG))
    mask_none = two(jnp.full((WIN, WIN), NEG, F32))
    mask_head = jnp.where(pl.program_id(2) > 0, mask_prev, mask_none)
    zeros = jnp.zeros((HEAD_DIM, WIN), BF16)
    ones = jnp.ones((BF16_ROWS, 2 * WIN), BF16)
    pad = jnp.zeros((LANES - HEADS, WIN), F32)
    units = [(rb, i, p) for rb in range(rb_count) for i in range(qb // WIN) for p in range(HEADS // 2)]

    def slices(i, p):
        return (slice(i * WIN, (i + 1) * WIN), slice((i - 1) * WIN, i * WIN),
                slice(p * 2 * HEAD_DIM, (p + 1) * 2 * HEAD_DIM))

    for n, (rb, i, p) in enumerate(units):
        cur, prev, cols = slices(i, p)
        k_prev = kp_ref[rb, :, cols] if i == 0 else kc_ref[rb, prev, cols]
        q2 = qt_ref[rb, cols, cur]
        qbd = jnp.concatenate(
            [jnp.concatenate([q2[:HEAD_DIM], zeros], axis=1),
             jnp.concatenate([zeros, q2[HEAD_DIM:]], axis=1)], axis=0)
        m_prev = mask_head if i == 0 else mask_prev
        s_scr[n, 0:WIN, :] = jnp.dot(jnp.concatenate([k_prev, eye], axis=1),
                                     jnp.concatenate([qbd, m_prev], axis=0), preferred_element_type=F32)
        s_scr[n, WIN:2 * WIN, :] = jnp.dot(jnp.concatenate([kc_ref[rb, cur, cols], eye], axis=1),
                                           jnp.concatenate([qbd, mask_cur], axis=0), preferred_element_type=F32)

    stats = []
    for n in range(len(units)):
        s = s_scr[n]
        m = jnp.max(s, axis=0, keepdims=True)
        e = jnp.exp2(s - m)
        p_scr[n] = e.astype(BF16)
        stats.append(m)

    outs, lses = {}, {}
    for n, (rb, i, p) in enumerate(units):
        cur, prev, cols = slices(i, p)
        v_prev = vp_ref[rb, cols, :] if i == 0 else vc_ref[rb, cols, prev]
        vv = jnp.concatenate([v_prev, vc_ref[rb, cols, cur]], axis=1)
        o2 = jnp.dot(jnp.concatenate([vv, ones], axis=0), p_scr[n], preferred_element_type=F32)
        m, den = stats[n], o2[2 * HEAD_DIM:2 * HEAD_DIM + 1, :]
        inv = 1.0 / den
        outs.setdefault((rb, i), []).extend(
            [o2[:HEAD_DIM, :WIN] * inv[:, :WIN], o2[HEAD_DIM:2 * HEAD_DIM, WIN:] * inv[:, WIN:]])
        lse = m * LN2 + jnp.log(den)
        lses.setdefault((rb, i), []).extend([lse[:, :WIN], lse[:, WIN:]])
    for (rb, i), pieces in outs.items():
        cur = slice(i * WIN, (i + 1) * WIN)
        o_ref[cur, rb * ATTN_WIDTH:(rb + 1) * ATTN_WIDTH] = jnp.concatenate(pieces, axis=0).T
        lse_ref[cur, rb * LANES:(rb + 1) * LANES] = jnp.concatenate(lses[(rb, i)] + [pad], axis=0).T


def _attention(qt, k, vt, dilation):
    rb, qb = _group_tiles(dilation)
    n_sub = SEQ // dilation
    per = qb // WIN
    prev_blk = lambda j: jnp.maximum(j * per - 1, 0)
    out, lse = pl.pallas_call(
        functools.partial(_attn_kernel, rb_count=rb, qb=qb),
        grid=(BATCH, dilation // rb, n_sub // qb),
        in_specs=[pl.BlockSpec((None, rb, ATTN_WIDTH, qb), lambda b, r, j: (b, r, 0, j)),
                  pl.BlockSpec((None, rb, qb, ATTN_WIDTH), lambda b, r, j: (b, r, j, 0)),
                  pl.BlockSpec((None, rb, WIN, ATTN_WIDTH), lambda b, r, j: (b, r, prev_blk(j), 0)),
                  pl.BlockSpec((None, rb, ATTN_WIDTH, qb), lambda b, r, j: (b, r, 0, j)),
                  pl.BlockSpec((None, rb, ATTN_WIDTH, WIN), lambda b, r, j: (b, r, 0, prev_blk(j)))],
        out_specs=[pl.BlockSpec((None, qb, rb * ATTN_WIDTH), lambda b, r, j: (b, j, r)),
                   pl.BlockSpec((None, qb, rb * LANES), lambda b, r, j: (b, j, r))],
        out_shape=[jax.ShapeDtypeStruct((BATCH, n_sub, dilation * ATTN_WIDTH), F32),
                   jax.ShapeDtypeStruct((BATCH, n_sub, dilation * LANES), F32)],
        scratch_shapes=[pltpu.VMEM((rb * per * (HEADS // 2), 2 * WIN, 2 * WIN), F32),
                        pltpu.VMEM((rb * per * (HEADS // 2), 2 * WIN, 2 * WIN), BF16)],
        compiler_params=_params(3),
        name=f"attn_d{dilation}",
    )(qt, k, k, vt, vt)
    return out.reshape(BATCH * n_sub, dilation * ATTN_WIDTH), lse.reshape(BATCH * n_sub, dilation * LANES)


def _combine_kernel(x_ref, o0_ref, o1_ref, o2_ref, l0_ref, l1_ref, l2_ref, ex_ref, w_ref, mg_ref, w1_ref, w2_ref,
                    out_ref, o_scr, l_scr):
    n_slab = ATTN_WIDTH // LANES
    for gi, (o_ref, l_ref) in enumerate(((o1_ref, l1_ref), (o2_ref, l2_ref))):
        d = ATTN_GROUPS[gi + 1][1]
        n = ROW_TILE // d
        for r in range(d):
            rows = pl.ds(r, n, stride=d)
            for c in range(n_slab):
                o_scr[gi, c, rows, :] = o_ref[:, r * ATTN_WIDTH + c * LANES: r * ATTN_WIDTH + (c + 1) * LANES]
            l_scr[gi, rows, :] = l_ref[:, r * LANES:(r + 1) * LANES]
    os = [o0_ref[...]] + [jnp.concatenate([o_scr[gi, c] for c in range(n_slab)], axis=1) for gi in range(2)]
    ls = [l0_ref[...], l_scr[0], l_scr[1]]
    m = jnp.maximum(jnp.maximum(ls[0], ls[1]), ls[2])
    es = [jnp.exp(l - m) for l in ls]
    inv = 1.0 / (es[0] + es[1] + es[2])
    mixed = jnp.zeros((ROW_TILE, ATTN_WIDTH), F32)
    for e, o in zip(es, os):
        wide = jnp.dot((e * inv).astype(BF16), ex_ref[...], preferred_element_type=F32)
        mixed = mixed + wide * o
    x_mid = x_ref[...] + jnp.dot(mixed.astype(BF16), w_ref[...], preferred_element_type=F32)
    out_ref[...] = _mlp_rows(x_mid, mg_ref, w1_ref, w2_ref)


def _combine_project_mlp(xt, outs, lses, expand, w_out, mlp_g, w1, w2, j, layer):
    t = xt.shape[0]
    assert ATTN_GROUPS[0][1] == 1
    dil = lambda w: [pl.BlockSpec((ROW_TILE // d, d * w), lambda i: (i, 0)) for _, d in ATTN_GROUPS]
    row = pl.BlockSpec((ROW_TILE, D_MODEL), lambda i: (i, 0))
    return pl.pallas_call(
        _combine_kernel,
        grid=(t // ROW_TILE,),
        in_specs=[row] + dil(ATTN_WIDTH) + dil(LANES)
                 + [_const_spec((LANES, ATTN_WIDTH)), _layer_spec((ATTN_WIDTH, D_MODEL), j)] + _mlp_specs(layer),
        out_specs=row,
        out_shape=jax.ShapeDtypeStruct(xt.shape, F32),
        scratch_shapes=[pltpu.VMEM((N_GROUPS - 1, ATTN_WIDTH // LANES, ROW_TILE, LANES), F32),
                        pltpu.VMEM((N_GROUPS - 1, ROW_TILE, LANES), F32)],
        compiler_params=_params(1),
        name="attn_combine_mlp",
    )(xt, *outs, *lses, expand, w_out, mlp_g.reshape(1, D_MODEL), w1, w2)


def _rope_tables():
    pos = jnp.arange(SEQ, dtype=F32)
    inv_freq = ROPE_THETA ** (-jnp.arange(0, HEAD_DIM, 2, dtype=F32) / HEAD_DIM)
    ang = pos[:, None] * inv_freq[None, :]
    return jnp.cos(ang), jnp.sin(ang)


def _dilated_table(tab, dilation):
    tt = _qkv_step_tokens(dilation)
    nl = tt // dilation
    return tab.reshape(SEQ // tt, nl, dilation, HALF).transpose(0, 3, 2, 1).reshape(SEQ // tt, HALF, tt)


def _attention_layer(xt, w_all, q_gain, k_gain, w_out, cos, sin, mlp_g, w1, w2, j, layer):
    x3 = xt.reshape(BATCH, SEQ, D_MODEL)
    lane_bcast = lambda v: jnp.broadcast_to(v.reshape(HEAD_DIM, 1), (HEAD_DIM, LANES))
    qg = lane_bcast(q_gain * (HEAD_DIM ** -0.5 * LOG2E))
    kg = lane_bcast(k_gain)
    head = jnp.arange(LANES)[:, None]
    lane = jnp.arange(ATTN_WIDTH)[None, :]
    expand = (lane // HEAD_DIM == head).astype(BF16)
    outs, lses = [], []
    for gi, (_, dilation) in enumerate(ATTN_GROUPS):
        qt, k, vt = _qkv_project(x3, w_all, j, gi, qg, kg, _dilated_table(cos, dilation),
                                 _dilated_table(sin, dilation), dilation)
        o, l = _attention(qt, k, vt, dilation)
        outs.append(o)
        lses.append(l)
    return _combine_project_mlp(xt, outs, lses, expand, w_out, mlp_g, w1, w2, j, layer)


def kernel(x, mixer_norm, mlp_norm, conv_w_in, conv_b_in, conv_w_dw, conv_b_dw, conv_ln_g, conv_ln_b,
           conv_w_out, conv_b_out, attn_w_in, attn_q_norm, attn_k_norm, attn_w_out, mlp_w1, mlp_w2):
    assert x.shape == (BATCH, SEQ, D_MODEL) and x.dtype == F32
    cos, sin = _rope_tables()
    w1_all, w2_all = mlp_w1.astype(BF16), mlp_w2.astype(BF16)
    conv_in_all, conv_out_all = conv_w_in.astype(BF16), conv_w_out.astype(BF16)
    attn_out_all = attn_w_out.astype(BF16)
    attn_gain = mixer_norm[1::2]
    attn_w_all = (attn_w_in * attn_gain[:, :, None]).astype(BF16)
    xt = x.reshape(BATCH * SEQ, D_MODEL)
    for i in range(DEPTH):
        j = i // 2
        if i % 2 == 0:
            x3 = _conv_layer(xt.reshape(BATCH, SEQ, D_MODEL), mixer_norm[i], conv_in_all, conv_b_in[j],
                             conv_w_dw[j], conv_b_dw[j], conv_ln_g[j], conv_ln_b[j], conv_out_all, conv_b_out[j],
                             mlp_norm[i], w1_all, w2_all, j, i)
            xt = x3.reshape(BATCH * SEQ, D_MODEL)
        else:
            xt = _attention_layer(xt, attn_w_all, attn_q_norm[j], attn_k_norm[j], attn_out_all, cos, sin,
                                  mlp_norm[i], w1_all, w2_all, j, i)
    return xt.reshape(BATCH, SEQ, D_MODEL)
```
